```python
import math
import jax, jax.numpy as jnp
from jax import lax
import numpy as np


D_MODEL = 2048
BATCH = 4
SEQ = 4096
DEPTH = 4

EXPAND = 2
D_MIX = EXPAND * D_MODEL
N_GROUPS = 4
GROUP = D_MIX // N_GROUPS
RWKV_HEAD = 64
RWKV_HEADS = GROUP // RWKV_HEAD
RWKV_LORA_W = 64
RWKV_LORA_A = 64
RWKV_GN_EPS = 64e-5
FOX_HEAD = 64
FOX_HEADS = GROUP // FOX_HEAD
GDN_HEAD = 128
GDN_HEADS = GROUP // GDN_HEAD
GDN_CONV = 4
GDN_CHUNK = 64
DIFF_HEAD = 64
DIFF_HEADS = GROUP // (2 * DIFF_HEAD)
Q_BLOCK = 128
NORM_EPS = 1e-6

N_RWKV = 3 * GROUP + RWKV_LORA_W + RWKV_LORA_A
N_FOX = 3 * GROUP + FOX_HEADS
N_GDN = 3 * GROUP + 2 * GDN_HEADS
N_DIFF = 3 * GROUP
N_IN = N_RWKV + N_FOX + N_GDN + N_DIFF + D_MIX

kernel_name = 'hybrid_rwkv7_fox_gdn_diff_block'


def rms_norm(x, g, eps=NORM_EPS):
    xf = x.astype(jnp.float32)
    y = xf * lax.rsqrt(jnp.mean(xf * xf, axis=-1, keepdims=True) + eps)
    return (y * g.astype(jnp.float32)).astype(x.dtype)


def l2_normalize(x, eps=1e-6):
    xf = x.astype(jnp.float32)
    return xf * lax.rsqrt(jnp.sum(xf * xf, axis=-1, keepdims=True) + eps)


def block_distance(q0, kv_len):
    return (q0 + jnp.arange(Q_BLOCK))[:, None] - jnp.arange(kv_len)[None, :]


def sweep_query_blocks(block_fn, seq_len):
    outs = [block_fn(i * Q_BLOCK, (i + 1) * Q_BLOCK) for i in range(seq_len // Q_BLOCK)]
    return jnp.concatenate(outs, axis=-2)


def rwkv7_step(state, inp):
    r, w, k, v, a, b = inp
    sa = jnp.einsum('bhij,bhj->bhi', state, a)
    state = state * w[:, :, None, :] + sa[..., None] * b[:, :, None, :] + v[..., None] * k[:, :, None, :]
    return state, jnp.einsum('bhij,bhj->bhi', state, r)


def rwkv7_time_mix(p, mu, w0, w_up, a0, a_up, k_k, k_a, r_k, ln_g, ln_b):
    bsz, seq, _ = p.shape
    f32 = jnp.float32
    prev = jnp.pad(p, ((0, 0), (1, 0), (0, 0)))[:, :-1]
    p = p + mu * (prev - p)
    r, k, v, w_lo, a_lo = jnp.split(p, [GROUP, 2 * GROUP, 3 * GROUP, 3 * GROUP + RWKV_LORA_W], axis=-1)
    w = (w0 + jnp.tanh(w_lo) @ w_up).astype(f32)
    decay = jnp.exp(-jnp.exp(-jax.nn.softplus(-w) - 0.5))
    a = jax.nn.sigmoid((a0 + a_lo @ a_up).astype(f32))
    heads = lambda t: t.astype(f32).reshape(bsz, seq, RWKV_HEADS, RWKV_HEAD)
    kf = k.astype(f32)
    kk = l2_normalize(heads(kf * k_k))
    k_mod = heads(kf * (1.0 + (a - 1.0) * k_a))
    r_h, v_h, a_h, decay_h = heads(r), heads(v), heads(a), heads(decay)
    xs = tuple(jnp.moveaxis(t, 1, 0) for t in (r_h, decay_h, k_mod, v_h, -kk, kk * a_h))
    state0 = jnp.zeros((bsz, RWKV_HEADS, RWKV_HEAD, RWKV_HEAD), f32)
    _, y = lax.scan(rwkv7_step, state0, xs)
    y = jnp.moveaxis(y, 0, 1)
    yc = y - jnp.mean(y, axis=-1, keepdims=True)
    y = yc * lax.rsqrt(jnp.mean(yc * yc, axis=-1, keepdims=True) + RWKV_GN_EPS)
    y = y.reshape(bsz, seq, GROUP) * ln_g + ln_b
    bonus = jnp.sum(r_h * k_mod * r_k, axis=-1, keepdims=True) * v_h
    return (y + bonus.reshape(bsz, seq, GROUP)).astype(p.dtype)


def forgetting_attention(p, f_b, q_g, k_g):
    bsz, seq, _ = p.shape
    f32 = jnp.float32
    q, k, v, f = jnp.split(p, [GROUP, 2 * GROUP, 3 * GROUP], axis=-1)
    shp = (bsz, seq, FOX_HEADS, FOX_HEAD)
    q = rms_norm(q.reshape(shp), q_g).transpose(0, 2, 1, 3)
    k = rms_norm(k.reshape(shp), k_g).transpose(0, 2, 1, 3)
    v = v.reshape(shp).transpose(0, 2, 1, 3)
    log_f = jax.nn.log_sigmoid((f + f_b).astype(f32))
    c = jnp.cumsum(log_f, axis=1).transpose(0, 2, 1)
    scale = FOX_HEAD ** -0.5

    def block(q0, kv_end):
        s = jnp.einsum('bhqd,bhkd->bhqk', q[:, :, q0:q0 + Q_BLOCK], k[:, :, :kv_end],
                       preferred_element_type=f32) * scale
        s = s + c[:, :, q0:q0 + Q_BLOCK, None] - c[:, :, None, :kv_end]
        s = jnp.where(block_distance(q0, kv_end) >= 0, s, -jnp.inf)
        prob = jax.nn.softmax(s, axis=-1)
        return jnp.einsum('bhqk,bhkd->bhqd', prob.astype(v.dtype), v[:, :, :kv_end])

    o = sweep_query_blocks(block, seq)
    return o.transpose(0, 2, 1, 3).reshape(bsz, seq, GROUP)


def causal_depthwise_conv(x, w):
    return lax.conv_general_dilated(x, w.astype(x.dtype)[:, None, :], window_strides=(1,),
                                    padding=[(w.shape[0] - 1, 0)],
                                    dimension_numbers=('NWC', 'WIO', 'NWC'),
                                    feature_group_count=x.shape[-1])


def chunk_gated_delta_rule(q, k, v, g, beta):
    bsz, seq, nh, dk = q.shape
    dv = v.shape[-1]
    n = seq // GDN_CHUNK
    chunks = lambda t: t.reshape(bsz, n, GDN_CHUNK, nh, -1).transpose(0, 3, 1, 2, 4)
    q, k, v = chunks(q), chunks(k), chunks(v)
    g = chunks(g[..., None])[..., 0]
    beta = chunks(beta[..., None])[..., 0]
    gc = jnp.cumsum(g, axis=-1)
    idx = jnp.arange(GDN_CHUNK)
    incl = idx[:, None] >= idx[None, :]
    strict = idx[:, None] > idx[None, :]
    decay = jnp.exp(jnp.where(incl, gc[..., :, None] - gc[..., None, :], -jnp.inf))
    k_beta = k * beta[..., None]
    m = jnp.where(strict, jnp.einsum('bhnid,bhnjd->bhnij', k_beta, k) * decay, 0.0)
    rhs = jnp.concatenate([v * beta[..., None], k_beta * jnp.exp(gc)[..., None]], axis=-1)
    sol = lax.linalg.triangular_solve(jnp.eye(GDN_CHUNK, dtype=m.dtype) + m, rhs,
                                      left_side=True, lower=True)
    u, w = sol[..., :dv], sol[..., dv:]
    a_intra = jnp.einsum('bhnid,bhnjd->bhnij', q, k) * decay

    def step(state, inp):
        q_c, k_c, u_c, w_c, gc_c, a_c = inp
        v_new = u_c - jnp.einsum('bhck,bhkv->bhcv', w_c, state)
        o = (jnp.einsum('bhck,bhkv->bhcv', q_c * jnp.exp(gc_c)[..., None], state)
             + jnp.einsum('bhcj,bhjv->bhcv', a_c, v_new))
        g_last = gc_c[..., -1:]
        state = (state * jnp.exp(g_last)[..., None]
                 + jnp.einsum('bhck,bhcv->bhkv', k_c * jnp.exp(g_last - gc_c)[..., None], v_new))
        return state, o

    xs = tuple(jnp.moveaxis(t, 2, 0) for t in (q, k, u, w, gc, a_intra))
    _, o = lax.scan(step, jnp.zeros((bsz, nh, dk, dv), jnp.float32), xs)
    return o.transpose(1, 0, 3, 2, 4).reshape(bsz, seq, nh, dv)


def gated_deltanet(p, conv_w, a_log, dt_bias, norm_g):
    bsz, seq, _ = p.shape
    f32 = jnp.float32
    qkv, a_logit, b_logit = jnp.split(p, [3 * GROUP, 3 * GROUP + GDN_HEADS], axis=-1)
    qkv = jax.nn.silu(causal_depthwise_conv(qkv, conv_w)).astype(f32)
    q, k, v = (t.reshape(bsz, seq, GDN_HEADS, GDN_HEAD) for t in jnp.split(qkv, 3, axis=-1))
    q = l2_normalize(q) * GDN_HEAD ** -0.5
    k = l2_normalize(k)
    g = -jnp.exp(a_log.astype(f32)) * jax.nn.softplus((a_logit + dt_bias).astype(f32))
    beta = jax.nn.sigmoid(b_logit.astype(f32))
    o = chunk_gated_delta_rule(q, k, v, g, beta)
    return rms_norm(o, norm_g).reshape(bsz, seq, GROUP).astype(p.dtype)


def differential_attention(p, layer, q_g, k_g, lq1, lk1, lq2, lk2, subln_g):
    bsz, seq, _ = p.shape
    f32 = jnp.float32
    q, k, v = jnp.split(p, 3, axis=-1)
    maps = lambda t, g: rms_norm(t.reshape(bsz, seq, DIFF_HEADS, 2, DIFF_HEAD), g).transpose(0, 2, 3, 1, 4)
    q, k = maps(q, q_g), maps(k, k_g)
    v = v.reshape(bsz, seq, DIFF_HEADS, 2 * DIFF_HEAD).transpose(0, 2, 1, 3)
    lam_init = 0.8 - 0.6 * math.exp(-0.3 * layer)
    lam = (jnp.exp(jnp.sum((lq1 * lk1).astype(f32))) - jnp.exp(jnp.sum((lq2 * lk2).astype(f32)))
           + lam_init)
    slopes = jnp.exp2(-8.0 * jnp.arange(1, DIFF_HEADS + 1, dtype=f32) / DIFF_HEADS)
    scale = DIFF_HEAD ** -0.5

    def block(q0, kv_end):
        dist = block_distance(q0, kv_end)
        alibi = -slopes[:, None, None] * dist.astype(f32)
        s = jnp.einsum('bhmqd,bhmkd->bhmqk', q[..., q0:q0 + Q_BLOCK, :], k[..., :kv_end, :],
                       preferred_element_type=f32) * scale + alibi[None, :, None]
        s = jnp.where(dist >= 0, s, -jnp.inf)
        prob = jax.nn.softmax(s, axis=-1)
        weights = prob[:, :, 0] - lam * prob[:, :, 1]
        return jnp.einsum('bhqk,bhkd->bhqd', weights.astype(v.dtype), v[:, :, :kv_end])

    o = sweep_query_blocks(block, seq)
    o = rms_norm(o, subln_g, eps=1e-5) * (1.0 - lam_init)
    return o.transpose(0, 2, 1, 3).reshape(bsz, seq, GROUP)


def setup_inputs(seed: int = 0) -> dict:
    key = jax.random.key(seed)
    ks = iter(jax.random.split(key, 32))
    nrm = lambda shape, s: s * jax.random.normal(next(ks), shape, jnp.float32)
    gain = lambda shape: 1.0 + nrm(shape, 0.02)
    dt = jnp.exp(jax.random.uniform(next(ks), (DEPTH, GDN_HEADS), jnp.float32,
                                    math.log(1e-3), math.log(1e-1)))
    return {
        'x': nrm((BATCH, SEQ, D_MODEL), 1.0),
        'norm_g': gain((DEPTH, D_MODEL)),
        'w_in': nrm((DEPTH, D_MODEL, N_IN), D_MODEL ** -0.5),
        'w_out': nrm((DEPTH, D_MIX, D_MODEL), D_MIX ** -0.5),
        'rwkv_mu': jax.random.uniform(next(ks), (DEPTH, N_RWKV), jnp.float32),
        'rwkv_w0': nrm((DEPTH, GROUP), 1.0),
        'rwkv_w_up': nrm((DEPTH, RWKV_LORA_W, GROUP), 0.1),
        'rwkv_a0': nrm((DEPTH, GROUP), 0.1),
        'rwkv_a_up': nrm((DEPTH, RWKV_LORA_A, GROUP), 0.5 * RWKV_LORA_A ** -0.5),
        'rwkv_k_k': 0.85 + nrm((DEPTH, GROUP), 0.02),
        'rwkv_k_a': gain((DEPTH, GROUP)),
        'rwkv_r_k': nrm((DEPTH, RWKV_HEADS, RWKV_HEAD), 0.1),
        'rwkv_ln_g': gain((DEPTH, GROUP)),
        'rwkv_ln_b': nrm((DEPTH, GROUP), 0.02),
        'fox_q_g': gain((DEPTH, FOX_HEAD)),
        'fox_k_g': gain((DEPTH, FOX_HEAD)),
        'fox_f_b': nrm((DEPTH, FOX_HEADS), 0.1),
        'gdn_conv': nrm((DEPTH, GDN_CONV, 3 * GROUP), GDN_CONV ** -0.5),
        'gdn_a_log': jnp.log(jax.random.uniform(next(ks), (DEPTH, GDN_HEADS), jnp.float32, 1.0, 16.0)),
        'gdn_dt_bias': dt + jnp.log(-jnp.expm1(-dt)),
        'gdn_norm_g': gain((DEPTH, GDN_HEAD)),
        'diff_q_g': gain((DEPTH, DIFF_HEAD)),
        'diff_k_g': gain((DEPTH, DIFF_HEAD)),
        'diff_lq1': nrm((DEPTH, DIFF_HEAD), 0.1),
        'diff_lk1': nrm((DEPTH, DIFF_HEAD), 0.1),
        'diff_lq2': nrm((DEPTH, DIFF_HEAD), 0.1),
        'diff_lk2': nrm((DEPTH, DIFF_HEAD), 0.1),
        'diff_subln_g': gain((DEPTH, 2 * DIFF_HEAD)),
    }


def reference(x, norm_g, w_in, w_out, rwkv_mu, rwkv_w0, rwkv_w_up, rwkv_a0, rwkv_a_up,
              rwkv_k_k, rwkv_k_a, rwkv_r_k, rwkv_ln_g, rwkv_ln_b, fox_q_g, fox_k_g, fox_f_b,
              gdn_conv, gdn_a_log, gdn_dt_bias, gdn_norm_g, diff_q_g, diff_k_g,
              diff_lq1, diff_lk1, diff_lq2, diff_lk2, diff_subln_g):
    bounds = [N_RWKV, N_RWKV + N_FOX, N_RWKV + N_FOX + N_GDN, N_RWKV + N_FOX + N_GDN + N_DIFF]
    for l in range(DEPTH):
        h = rms_norm(x, norm_g[l])
        p = jnp.einsum('bsd,dn->bsn', h, w_in[l])
        p_rwkv, p_fox, p_gdn, p_diff, z = jnp.split(p, bounds, axis=-1)
        y = jnp.concatenate([
            rwkv7_time_mix(p_rwkv, rwkv_mu[l], rwkv_w0[l], rwkv_w_up[l], rwkv_a0[l], rwkv_a_up[l],
                           rwkv_k_k[l], rwkv_k_a[l], rwkv_r_k[l], rwkv_ln_g[l], rwkv_ln_b[l]),
            forgetting_attention(p_fox, fox_f_b[l], fox_q_g[l], fox_k_g[l]),
            gated_deltanet(p_gdn, gdn_conv[l], gdn_a_log[l], gdn_dt_bias[l], gdn_norm_g[l]),
            differential_attention(p_diff, l, diff_q_g[l], diff_k_g[l], diff_lq1[l], diff_lk1[l],
                                   diff_lq2[l], diff_lk2[l], diff_subln_g[l]),
        ], axis=-1)
        x = x + jnp.einsum('bsm,md->bsd', y * jax.nn.silu(z), w_out[l])
    return x
```

```python
import functools
import math

import jax
import jax.numpy as jnp
from jax import lax
from jax.experimental import pallas as pl
from jax.experimental.pallas import tpu as pltpu

F32 = jnp.float32
BF16 = jnp.bfloat16
HI = lax.Precision.HIGHEST

D_MODEL = 2048
DEPTH = 4
GROUP = 1024
D_MIX = 4 * GROUP
LANES = 128
HEAD64 = 64
CHUNK = 64
LORA = 64
RWKV_GN_EPS = 64e-5
NORM_EPS = 1e-6
GDN_CONV = 4
N_RWKV = 3 * GROUP + 2 * LORA
N_FOX = 3 * GROUP + 16
N_GDN = 3 * GROUP + 16
N_DIFF = 3 * GROUP
SEG_PAD = 3200
VMEM_LIMIT = 56 * 1024 * 1024

NT = (((1,), (1,)), ((), ()))
TN = (((0,), (0,)), ((), ()))


def _params(n_axes):
    return pltpu.CompilerParams(dimension_semantics=("arbitrary",) * n_axes,
                                vmem_limit_bytes=VMEM_LIMIT)


def _softplus(x):
    return jnp.maximum(x, 0.0) + jnp.log1p(jnp.exp(-jnp.abs(x)))


def _iota(shape, axis):
    return lax.broadcasted_iota(jnp.int32, shape, axis)


def _half_sum(x, low):
    s0 = jnp.sum(jnp.where(low, x, 0.0), axis=1, keepdims=True)
    s1 = jnp.sum(jnp.where(low, 0.0, x), axis=1, keepdims=True)
    return jnp.where(low, s0, s1)


def _neumann_inverse(n):
    eye = (_iota((CHUNK, CHUNK), 0) == _iota((CHUNK, CHUNK), 1)).astype(F32)
    t = eye + n
    p = n
    for _ in range(5):
        p = jnp.dot(p, p, precision=HI, preferred_element_type=F32)
        t = t + jnp.dot(t, p, precision=HI, preferred_element_type=F32)
    return t


def _rmsnorm_body(x_ref, g_ref, o_ref):
    x = x_ref[...]
    ms = jnp.mean(x * x, axis=-1, keepdims=True)
    o_ref[...] = (x * lax.rsqrt(ms + NORM_EPS) * g_ref[...]).astype(o_ref.dtype)


def _rmsnorm(x2d, g, tm=512):
    m, d = x2d.shape
    return pl.pallas_call(
        _rmsnorm_body, grid=(m // tm,),
        in_specs=[pl.BlockSpec((tm, d), lambda i: (i, 0)), pl.BlockSpec((1, d), lambda i: (0, 0))],
        out_specs=pl.BlockSpec((tm, d), lambda i: (i, 0)),
        out_shape=jax.ShapeDtypeStruct((m, d), BF16), compiler_params=_params(1),
        name="rmsnorm")(x2d, g.reshape(1, d))


def _matmul_body(h_ref, w_ref, o_ref, *, silu):
    acc = jnp.dot(h_ref[...], w_ref[...], preferred_element_type=F32)
    if silu:
        acc = acc * jax.nn.sigmoid(acc)
    o_ref[...] = acc.astype(o_ref.dtype)


def _matmul(h, w, tn, out_dtype, silu=False, tm=1024, name="in_proj"):
    m, k = h.shape
    n = w.shape[1]
    tm = min(tm, m)
    return pl.pallas_call(
        functools.partial(_matmul_body, silu=silu), grid=(m // tm, n // tn),
        in_specs=[pl.BlockSpec((tm, k), lambda i, j: (i, 0)), pl.BlockSpec((k, tn), lambda i, j: (0, j))],
        out_specs=pl.BlockSpec((tm, tn), lambda i, j: (i, j)),
        out_shape=jax.ShapeDtypeStruct((m, n), out_dtype), compiler_params=_params(2),
        name=name)(h, w)


def _outproj_body(x_ref, yr_ref, yf_ref, yg_ref, yd_ref, sz_ref, w_ref, o_ref):
    acc = x_ref[...]
    for g, y_ref in enumerate((yr_ref, yf_ref, yg_ref, yd_ref)):
        gate = (y_ref[...].astype(F32) * sz_ref[:, g * GROUP:(g + 1) * GROUP].astype(F32)).astype(BF16)
        acc = acc + jnp.dot(gate, w_ref[g * GROUP:(g + 1) * GROUP, :], preferred_element_type=F32)
    o_ref[...] = acc


def _outproj(x2d, ys, sz, w_out, tm=512, tn=1024):
    m, d = x2d.shape
    tm = min(tm, m)
    yspec = pl.BlockSpec((tm, GROUP), lambda i, j: (i, 0))
    return pl.pallas_call(
        _outproj_body, grid=(m // tm, d // tn),
        in_specs=[pl.BlockSpec((tm, tn), lambda i, j: (i, j)), yspec, yspec, yspec, yspec,
                  pl.BlockSpec((tm, D_MIX), lambda i, j: (i, 0)),
                  pl.BlockSpec((D_MIX, tn), lambda i, j: (0, j))],
        out_specs=pl.BlockSpec((tm, tn), lambda i, j: (i, j)),
        out_shape=jax.ShapeDtypeStruct((m, d), F32), compiler_params=_params(2),
        name="out_proj")(x2d, *ys, sz, w_out)


def _attend(qm, kn_ref, vb_ref, bias_fn, qi, tq):
    causal = _iota((tq, tq), 0) >= _iota((tq, tq), 1)

    def step(j, carry, diagonal):
        m, l, acc = carry
        k0 = pl.multiple_of(j * tq, tq)
        s = lax.dot_general(qm, kn_ref[pl.ds(k0, tq), :], NT, preferred_element_type=F32)
        s = s + bias_fn(j)
        if diagonal:
            s = jnp.where(causal, s, -jnp.inf)
        m_new = jnp.maximum(m, jnp.max(s, axis=1, keepdims=True))
        alpha = jnp.exp(m - m_new)
        p = jnp.exp(s - m_new)
        l = alpha * l + jnp.sum(p, axis=1, keepdims=True)
        acc = alpha * acc + jnp.dot(p.astype(BF16), vb_ref[pl.ds(k0, tq), :], preferred_element_type=F32)
        return m_new, l, acc

    init = (jnp.full((tq, 1), -jnp.inf, F32), jnp.zeros((tq, 1), F32), jnp.zeros((tq, LANES), F32))
    carry = lax.fori_loop(0, qi, lambda j, c: step(j, c, False), init)
    return step(qi, carry, True)


def _half_rmsnorm(x, gain, low):
    ms = _half_sum(x * x, low) * (1.0 / HEAD64)
    return x * lax.rsqrt(ms + NORM_EPS) * gain


def _stage_keys(k_ref, v_ref, kg_ref, kn_s, vb_s, low, seq):
    rows = 512 if seq % 512 == 0 else seq

    def body(i, c):
        r0 = pl.multiple_of(i * rows, rows)
        kn_s[pl.ds(r0, rows), :] = _half_rmsnorm(k_ref[0, pl.ds(r0, rows), :], kg_ref[...], low).astype(BF16)
        vb_s[pl.ds(r0, rows), :] = v_ref[0, pl.ds(r0, rows), :].astype(BF16)
        return c

    lax.fori_loop(0, seq // rows, body, 0)


def _fox_body(q_ref, k_ref, v_ref, c_ref, qg_ref, kg_ref, o_ref, kn_s, vb_s, *, seq, tq):
    hp = pl.program_id(1)
    qi = pl.program_id(2)
    low = _iota((1, LANES), 1) < HEAD64

    @pl.when(qi == 0)
    def _():
        _stage_keys(k_ref, v_ref, kg_ref, kn_s, vb_s, low, seq)

    qn = _half_rmsnorm(q_ref[0], qg_ref[...], low) * (HEAD64 ** -0.5)
    outs = []
    for h in range(2):
        qm = jnp.where(low if h == 0 else ~low, qn, 0.0).astype(BF16)
        row = 2 * hp + h
        c_q0 = c_ref[0, row, pl.ds(qi, 1), :][:, 0:1]
        m, l, acc = _attend(qm, kn_s, vb_s, lambda j: c_q0 - c_ref[0, row, pl.ds(j, 1), :], qi, tq)
        outs.append(acc / l)
    o_ref[0] = jnp.where(low, outs[0], outs[1]).astype(o_ref.dtype)


def _fox_cum_body(f_ref, fb_ref, c_ref, *, seq):
    x = f_ref[0] + fb_ref[...]
    logf = jnp.minimum(x, 0.0) - jnp.log1p(jnp.exp(-jnp.abs(x)))
    upper = (_iota((LANES, LANES), 0) <= _iota((LANES, LANES), 1)).astype(F32)
    carry = jnp.zeros((16, 1), F32)
    for blk in range(seq // LANES):
        sl = slice(blk * LANES, (blk + 1) * LANES)
        cb = jnp.dot(logf[:, sl], upper, precision=HI, preferred_element_type=F32) + carry
        c_ref[0, :, sl] = cb
        carry = cb[:, LANES - 1:LANES]


def _fox(p_fox, f_b, q_g, k_g, tq=256):
    bsz, seq, _ = p_fox.shape
    tq = min(tq, seq)
    f_t = jnp.transpose(p_fox[:, :, 3 * GROUP:3 * GROUP + 16], (0, 2, 1))
    c = pl.pallas_call(
        functools.partial(_fox_cum_body, seq=seq), grid=(bsz,),
        in_specs=[pl.BlockSpec((1, 16, seq), lambda b: (b, 0, 0)), pl.BlockSpec((16, 1), lambda b: (0, 0))],
        out_specs=pl.BlockSpec((1, 16, seq), lambda b: (b, 0, 0)),
        out_shape=jax.ShapeDtypeStruct((bsz, 16, seq), F32), compiler_params=_params(1),
        name="fox_cumsum")(f_t, f_b.reshape(16, 1))
    c = c.reshape(bsz, 16, seq // tq, tq)
    nb = GROUP // LANES
    gain = lambda g: jnp.tile(g, 2).reshape(1, LANES)
    return pl.pallas_call(
        functools.partial(_fox_body, seq=seq, tq=tq), grid=(bsz, nb, seq // tq),
        in_specs=[pl.BlockSpec((1, tq, LANES), lambda b, h, i: (b, i, h)),
                  pl.BlockSpec((1, seq, LANES), lambda b, h, i: (b, 0, nb + h)),
                  pl.BlockSpec((1, seq, LANES), lambda b, h, i: (b, 0, 2 * nb + h)),
                  pl.BlockSpec((1, 16, seq // tq, tq), lambda b, h, i: (b, 0, 0, 0)),
                  pl.BlockSpec((1, LANES), lambda b, h, i: (0, 0)),
                  pl.BlockSpec((1, LANES), lambda b, h, i: (0, 0))],
        out_specs=pl.BlockSpec((1, tq, LANES), lambda b, h, i: (b, i, h)),
        out_shape=jax.ShapeDtypeStruct((bsz, seq, GROUP), BF16),
        scratch_shapes=[pltpu.VMEM((seq, LANES), BF16), pltpu.VMEM((seq, LANES), BF16)],
        compiler_params=_params(3), name="fox_attention")(p_fox, p_fox, p_fox, c, gain(q_g), gain(k_g))


def _diff_body(q_ref, k_ref, v_ref, qg_ref, kg_ref, lam_ref, sg_ref, o_ref, kn_s, vb_s, *, seq, tq, lam_init):
    head = pl.program_id(1)
    qi = pl.program_id(2)
    low = _iota((1, LANES), 1) < HEAD64

    @pl.when(qi == 0)
    def _():
        _stage_keys(k_ref, v_ref, kg_ref, kn_s, vb_s, low, seq)

    lq1, lk1, lq2, lk2 = (lam_ref[i:i + 1, :] for i in range(4))
    lam = (jnp.exp(jnp.sum(lq1 * lk1, axis=1, keepdims=True))
           - jnp.exp(jnp.sum(lq2 * lk2, axis=1, keepdims=True)) + lam_init)
    slope = jnp.exp2(-(head + 1).astype(F32) * jnp.ones((1, 1), F32))
    key_pos = _iota((1, tq), 1).astype(F32)

    def alibi(j):
        return slope * (key_pos + ((j - qi) * tq).astype(F32))

    qn = _half_rmsnorm(q_ref[0], qg_ref[...], low) * (HEAD64 ** -0.5)
    outs = []
    for h in range(2):
        qm = jnp.where(low if h == 0 else ~low, qn, 0.0).astype(BF16)
        m, l, acc = _attend(qm, kn_s, vb_s, alibi, qi, tq)
        outs.append(acc / l)
    o = outs[0] - lam * outs[1]
    ms = jnp.mean(o * o, axis=1, keepdims=True)
    o_ref[0] = (o * lax.rsqrt(ms + 1e-5) * sg_ref[...] * (1.0 - lam_init)).astype(o_ref.dtype)


def _diff(p_diff, layer, q_g, k_g, lq1, lk1, lq2, lk2, subln_g, tq=256):
    bsz, seq, _ = p_diff.shape
    tq = min(tq, seq)
    nb = GROUP // LANES
    lam_init = 0.8 - 0.6 * math.exp(-0.3 * layer)
    gain = lambda g: jnp.tile(g, 2).reshape(1, LANES)
    lam_rows = jnp.stack([lq1, lk1, lq2, lk2])
    return pl.pallas_call(
        functools.partial(_diff_body, seq=seq, tq=tq, lam_init=lam_init), grid=(bsz, nb, seq // tq),
        in_specs=[pl.BlockSpec((1, tq, LANES), lambda b, h, i: (b, i, h)),
                  pl.BlockSpec((1, seq, LANES), lambda b, h, i: (b, 0, nb + h)),
                  pl.BlockSpec((1, seq, LANES), lambda b, h, i: (b, 0, 2 * nb + h)),
                  pl.BlockSpec((1, LANES), lambda b, h, i: (0, 0)),
                  pl.BlockSpec((1, LANES), lambda b, h, i: (0, 0)),
                  pl.BlockSpec((4, HEAD64), lambda b, h, i: (0, 0)),
                  pl.BlockSpec((1, LANES), lambda b, h, i: (0, 0))],
        out_specs=pl.BlockSpec((1, tq, LANES), lambda b, h, i: (b, i, h)),
        out_shape=jax.ShapeDtypeStruct((bsz, seq, GROUP), BF16),
        scratch_shapes=[pltpu.VMEM((seq, LANES), BF16), pltpu.VMEM((seq, LANES), BF16)],
        compiler_params=_params(3), name="diff_attention")(
            p_diff, p_diff, p_diff, gain(q_g), gain(k_g), lam_rows, subln_g.reshape(1, LANES))


def _gdn_body(q_ref, k_ref, v_ref, gate_ref, cwq_ref, cwk_ref, cwv_ref, alog_ref, dt_ref, ng_ref, o_ref,
              u_s, w_s, qg_s, kd_s, ai_s, eg_s, *, seq):
    head = pl.program_id(1)
    n_chunks = seq // CHUNK
    lane = _iota((1, LANES), 1)
    pick = lambda x, idx: jnp.sum(jnp.where(lane == idx, x, 0.0), axis=1, keepdims=True)
    a_scale = -jnp.exp(pick(alog_ref[...], head))
    dt_bias = pick(dt_ref[...], head)
    ri = _iota((CHUNK, CHUNK), 0)
    ci = _iota((CHUNK, CHUNK), 1)
    eye, tril, strict = ri == ci, ri >= ci, ri > ci

    def conv_silu(ref, cw_ref, r0, rp, not_first):
        cur = ref[0, pl.ds(r0, CHUNK), :]
        ext = jnp.concatenate([ref[0, pl.ds(rp, CHUNK), :] * not_first, cur], axis=0)
        acc = cur * cw_ref[GDN_CONV - 1:GDN_CONV, :]
        for j in range(1, GDN_CONV):
            acc = acc + pltpu.roll(ext, j, 0)[CHUNK:, :] * cw_ref[GDN_CONV - 1 - j:GDN_CONV - j, :]
        return acc * jax.nn.sigmoid(acc)

    def prepare(c, carry):
        r0 = pl.multiple_of(c * CHUNK, CHUNK)
        rp = pl.multiple_of(jnp.maximum(c - 1, 0) * CHUNK, CHUNK)
        not_first = (c > 0).astype(F32)
        q = conv_silu(q_ref, cwq_ref, r0, rp, not_first)
        k = conv_silu(k_ref, cwk_ref, r0, rp, not_first)
        v = conv_silu(v_ref, cwv_ref, r0, rp, not_first)
        q = q * lax.rsqrt(jnp.sum(q * q, axis=1, keepdims=True) + 1e-6) * (LANES ** -0.5)
        k = k * lax.rsqrt(jnp.sum(k * k, axis=1, keepdims=True) + 1e-6)
        gates = gate_ref[0, pl.ds(r0, CHUNK), :]
        g_col = a_scale * _softplus(pick(gates, head) + dt_bias)
        beta = jax.nn.sigmoid(pick(gates, head + 8))
        g_row = jnp.sum(jnp.where(eye, g_col, 0.0), axis=0, keepdims=True)
        gc_col = jnp.sum(jnp.where(tril, g_row, 0.0), axis=1, keepdims=True)
        gc_row = jnp.sum(jnp.where(ri <= ci, g_col, 0.0), axis=0, keepdims=True)
        decay = jnp.exp(jnp.where(tril, gc_col - gc_row, -jnp.inf))
        kb = k * beta
        kk = lax.dot_general(kb, k, NT, precision=HI, preferred_element_type=F32)
        t = _neumann_inverse(-jnp.where(strict, kk * decay, 0.0))
        exp_gc = jnp.exp(gc_col)
        g_last = gc_col[CHUNK - 1:CHUNK, :]
        u_s[pl.ds(r0, CHUNK), :] = jnp.dot(t, v * beta, precision=HI, preferred_element_type=F32)
        w_s[pl.ds(r0, CHUNK), :] = jnp.dot(t, kb * exp_gc, precision=HI, preferred_element_type=F32)
        qg_s[pl.ds(r0, CHUNK), :] = q * exp_gc
        kd_s[pl.ds(r0, CHUNK), :] = k * jnp.exp(g_last - gc_col)
        ai_s[pl.ds(r0, CHUNK), :] = lax.dot_general(q, k, NT, preferred_element_type=F32) * decay
        eg_s[pl.ds(c, 1), :] = jnp.broadcast_to(jnp.exp(g_last), (1, LANES))
        return carry

    lax.fori_loop(0, n_chunks, prepare, 0)

    def recur(c, state):
        r0 = pl.multiple_of(c * CHUNK, CHUNK)
        sb = state.astype(BF16)
        v_new = u_s[pl.ds(r0, CHUNK), :] - jnp.dot(w_s[pl.ds(r0, CHUNK), :].astype(BF16), sb,
                                                   preferred_element_type=F32)
        vb = v_new.astype(BF16)
        o = (jnp.dot(qg_s[pl.ds(r0, CHUNK), :].astype(BF16), sb, preferred_element_type=F32)
             + jnp.dot(ai_s[pl.ds(r0, CHUNK), :].astype(BF16), vb, preferred_element_type=F32))
        state = state * eg_s[pl.ds(c, 1), :] + lax.dot_general(
            kd_s[pl.ds(r0, CHUNK), :].astype(BF16), vb, TN, preferred_element_type=F32)
        ms = jnp.mean(o * o, axis=1, keepdims=True)
        o_ref[0, pl.ds(r0, CHUNK), :] = (o * lax.rsqrt(ms + NORM_EPS) * ng_ref[...]).astype(o_ref.dtype)
        return state

    lax.fori_loop(0, n_chunks, recur, jnp.zeros((LANES, LANES), F32))


def _gdn(p_gdn, conv_w, a_log, dt_bias, norm_g):
    bsz, seq, _ = p_gdn.shape
    nb = GROUP // LANES
    pad = lambda v: jnp.pad(v, (0, LANES - v.shape[0])).reshape(1, LANES)
    seq_spec = lambda off: pl.BlockSpec((1, seq, LANES), lambda b, h: (b, 0, off + h))
    cw_spec = lambda off: pl.BlockSpec((GDN_CONV, LANES), lambda b, h: (0, off + h))
    row_spec = pl.BlockSpec((1, LANES), lambda b, h: (0, 0))
    return pl.pallas_call(
        functools.partial(_gdn_body, seq=seq), grid=(bsz, nb),
        in_specs=[seq_spec(0), seq_spec(nb), seq_spec(2 * nb),
                  pl.BlockSpec((1, seq, LANES), lambda b, h: (b, 0, 3 * nb)),
                  cw_spec(0), cw_spec(nb), cw_spec(2 * nb), row_spec, row_spec, row_spec],
        out_specs=pl.BlockSpec((1, seq, LANES), lambda b, h: (b, 0, h)),
        out_shape=jax.ShapeDtypeStruct((bsz, seq, GROUP), BF16),
        scratch_shapes=[pltpu.VMEM((seq, LANES), F32)] * 4
        + [pltpu.VMEM((seq, CHUNK), F32), pltpu.VMEM((max(seq // CHUNK, 8), LANES), F32)],
        compiler_params=_params(2), name="gated_deltanet")(
            p_gdn, p_gdn, p_gdn, p_gdn, conv_w, conv_w, conv_w, pad(a_log), pad(dt_bias),
            norm_g.reshape(1, LANES))


def _rwkv_prep_body(p_ref, halo_ref, mu_ref, w0_ref, wup_ref, a0_ref, aup_ref, kk_ref, ka_ref,
                    r_ref, lw_ref, k_ref, v_ref, kkn_ref, kka_ref):
    p = p_ref[0]
    first = pl.program_id(1) == 0
    last_prev = jnp.where(first, 0.0, halo_ref[0, 7:8, :])
    prev = jnp.where(_iota((p.shape[0], 1), 0) == 0, last_prev, pltpu.roll(p, 1, 0))
    xs = p + mu_ref[...] * (prev - p)
    r, k, v = xs[:, :GROUP], xs[:, GROUP:2 * GROUP], xs[:, 2 * GROUP:3 * GROUP]
    lora = xs[:, 3 * GROUP:]
    w = w0_ref[...] + jnp.dot(jnp.tanh(lora).astype(BF16), wup_ref[...], preferred_element_type=F32)
    a = jax.nn.sigmoid(a0_ref[...] + jnp.dot(lora.astype(BF16), aup_ref[...], preferred_element_type=F32))
    low = _iota((1, LANES), 1) < HEAD64
    r_ref[0] = r
    v_ref[0] = v
    lw_ref[0] = -jnp.exp(-_softplus(-w) - 0.5)
    k_ref[0] = k * (1.0 + (a - 1.0) * ka_ref[...])
    for blk in range(GROUP // LANES):
        sl = slice(blk * LANES, (blk + 1) * LANES)
        kk0 = k[:, sl] * kk_ref[:, sl]
        kkn = kk0 * lax.rsqrt(_half_sum(kk0 * kk0, low) + 1e-6)
        kkn_ref[0, :, sl] = kkn
        kka_ref[0, :, sl] = kkn * a[:, sl]


def _rwkv_body(r_ref, lw_ref, k_ref, v_ref, kk_ref, kka_ref, rk_ref, lng_ref, lnb_ref, o_ref,
               ar_s, t_s, arb_s, w1_s, y1_s, bk_s, gc_s, state_s, *, ts):
    n_chunks = ts // CHUNK
    low = _iota((1, LANES), 1) < HEAD64
    ri = _iota((CHUNK, CHUNK), 0)
    ci = _iota((CHUNK, CHUNK), 1)
    tril, strict = ri >= ci, ri > ci
    tril_f = tril.astype(F32)
    block_diag = (_iota((LANES, LANES), 0) < HEAD64) == (_iota((LANES, LANES), 1) < HEAD64)

    @pl.when(pl.program_id(2) == 0)
    def _():
        state_s[...] = jnp.zeros_like(state_s)

    def prepare(c, carry):
        r0 = pl.multiple_of(c * CHUNK, CHUNK)
        r2 = pl.multiple_of(c * 2 * CHUNK, 2 * CHUNK)
        rows = pl.ds(r0, CHUNK)
        lw = lw_ref[0, rows, :]
        v = v_ref[0, rows, :]
        cum = jnp.dot(tril_f, lw, precision=HI, preferred_element_type=F32)
        gam = jnp.exp(cum)
        inv = jnp.exp(-cum)
        g_end = gam[CHUNK - 1:CHUNK, :]
        at = -kk_ref[0, rows, :] * jnp.exp(cum - lw)
        rt = r_ref[0, rows, :] * gam
        bt = kka_ref[0, rows, :] * inv
        kt = k_ref[0, rows, :] * inv
        ts_, arbs, w1s, y1s = [], [], [], []
        for h in range(2):
            mh = low if h == 0 else ~low
            lhs = jnp.concatenate([jnp.where(mh, at, 0.0), jnp.where(mh, rt, 0.0)], axis=0)
            pb = lax.dot_general(lhs, bt, NT, precision=HI, preferred_element_type=F32)
            pk = lax.dot_general(lhs, kt, NT, precision=HI, preferred_element_type=F32)
            t = _neumann_inverse(jnp.where(strict, pb[:CHUNK], 0.0))
            akv = jnp.dot(jnp.where(strict, pk[:CHUNK], 0.0), v, precision=HI, preferred_element_type=F32)
            ts_.append(t)
            arbs.append(jnp.where(tril, pb[CHUNK:], 0.0))
            w1s.append(jnp.dot(t, akv, precision=HI, preferred_element_type=F32))
            y1s.append(jnp.dot(jnp.where(tril, pk[CHUNK:], 0.0), v, precision=HI, preferred_element_type=F32))
        ar_s[pl.ds(r2, 2 * CHUNK), :] = jnp.concatenate([at, rt], axis=0)
        bk_s[pl.ds(r2, 2 * CHUNK), :] = jnp.concatenate([bt * g_end, kt * g_end], axis=0)
        t_s[rows, :] = jnp.concatenate(ts_, axis=1)
        arb_s[rows, :] = jnp.concatenate(arbs, axis=1)
        w1_s[rows, :] = jnp.where(low, w1s[0], w1s[1])
        y1_s[rows, :] = jnp.where(low, y1s[0], y1s[1])
        gc_s[pl.ds(c, 1), :] = g_end
        return carry

    lax.fori_loop(0, n_chunks, prepare, 0)

    def split_heads(x):
        return jnp.concatenate([jnp.where(low, x, 0.0), jnp.where(low, 0.0, x)], axis=0).astype(BF16)

    def recur(c, state):
        r0 = pl.multiple_of(c * CHUNK, CHUNK)
        r2 = pl.multiple_of(c * 2 * CHUNK, 2 * CHUNK)
        rows = pl.ds(r0, CHUNK)
        v = v_ref[0, rows, :]
        xg = lax.dot_general(ar_s[pl.ds(r2, 2 * CHUNK), :].astype(BF16), state.astype(BF16), NT,
                             preferred_element_type=F32)
        u = jnp.dot(t_s[rows, :].astype(BF16), split_heads(xg[:CHUNK]), preferred_element_type=F32) + w1_s[rows, :]
        y = (xg[CHUNK:] + y1_s[rows, :]
             + jnp.dot(arb_s[rows, :].astype(BF16), split_heads(u), preferred_element_type=F32))
        upd = lax.dot_general(jnp.concatenate([u, v], axis=0).astype(BF16),
                              bk_s[pl.ds(r2, 2 * CHUNK), :].astype(BF16), TN, preferred_element_type=F32)
        state = state * gc_s[pl.ds(c, 1), :] + jnp.where(block_diag, upd, 0.0)
        mean = _half_sum(y, low) * (1.0 / HEAD64)
        yc = y - mean
        var = _half_sum(yc * yc, low) * (1.0 / HEAD64)
        rkk = r_ref[0, rows, :] * k_ref[0, rows, :] * rk_ref[...]
        out = yc * lax.rsqrt(var + RWKV_GN_EPS) * lng_ref[...] + lnb_ref[...] + _half_sum(rkk, low) * v
        o_ref[0, rows, :] = out.astype(o_ref.dtype)
        return state

    state_s[...] = lax.fori_loop(0, n_chunks, recur, state_s[...])


def _rwkv(p_rwkv, mu, w0, w_up, a0, a_up, k_k, k_a, r_k, ln_g, ln_b, tm=256, ts=1024):
    bsz, seq, _ = p_rwkv.shape
    tm, ts = min(tm, seq), min(ts, seq)
    nb = GROUP // LANES
    zeros = jnp.zeros((LORA, GROUP), F32)
    wup = jnp.concatenate([w_up, zeros], axis=0).astype(BF16)
    aup = jnp.concatenate([zeros, a_up], axis=0).astype(BF16)
    row = lambda v: v.reshape(1, -1)
    full = lambda n: pl.BlockSpec((1, n), lambda b, i: (0, 0))
    out_spec = pl.BlockSpec((1, tm, GROUP), lambda b, i: (b, i, 0))
    seq_f32 = jax.ShapeDtypeStruct((bsz, seq, GROUP), F32)
    r, lw, k, v, kk, kka = pl.pallas_call(
        _rwkv_prep_body, grid=(bsz, seq // tm),
        in_specs=[pl.BlockSpec((1, tm, N_RWKV), lambda b, i: (b, i, 0)),
                  pl.BlockSpec((1, 8, N_RWKV), lambda b, i: (b, jnp.maximum(i * (tm // 8) - 1, 0), 0)),
                  full(N_RWKV), full(GROUP),
                  pl.BlockSpec((2 * LORA, GROUP), lambda b, i: (0, 0)), full(GROUP),
                  pl.BlockSpec((2 * LORA, GROUP), lambda b, i: (0, 0)), full(GROUP), full(GROUP)],
        out_specs=[out_spec] * 6, out_shape=[seq_f32] * 6, compiler_params=_params(2),
        name="rwkv_prep")(p_rwkv, p_rwkv, row(mu), row(w0), wup, row(a0), aup, row(k_k), row(k_a))
    n_chunks = ts // CHUNK
    seq_spec = pl.BlockSpec((1, ts, LANES), lambda b, h, s: (b, s, h))
    par_spec = pl.BlockSpec((1, LANES), lambda b, h, s: (0, h))
    return pl.pallas_call(
        functools.partial(_rwkv_body, ts=ts), grid=(bsz, nb, seq // ts),
        in_specs=[seq_spec] * 6 + [par_spec] * 3,
        out_specs=seq_spec, out_shape=jax.ShapeDtypeStruct((bsz, seq, GROUP), BF16),
        scratch_shapes=[pltpu.VMEM((2 * ts, LANES), F32), pltpu.VMEM((ts, LANES), F32),
                        pltpu.VMEM((ts, LANES), F32), pltpu.VMEM((ts, LANES), F32),
                        pltpu.VMEM((ts, LANES), F32), pltpu.VMEM((2 * ts, LANES), F32),
                        pltpu.VMEM((max(n_chunks, 8), LANES), F32), pltpu.VMEM((LANES, LANES), F32)],
        compiler_params=_params(3), name="rwkv7")(
            r, lw, k, v, kk, kka, row(r_k), row(ln_g), row(ln_b))


def _split_w_in(w_in):
    b0, b1, b2, b3 = N_RWKV, N_RWKV + N_FOX, N_RWKV + N_FOX + N_GDN, N_RWKV + N_FOX + N_GDN + N_DIFF
    pad = lambda w: jnp.pad(w, ((0, 0), (0, 0), (0, SEG_PAD - w.shape[-1])))
    wb = w_in.astype(BF16)
    return wb[..., :b0], pad(wb[..., b0:b1]), pad(wb[..., b1:b2]), wb[..., b2:b3], wb[..., b3:]


def kernel(x, norm_g, w_in, w_out, rwkv_mu, rwkv_w0, rwkv_w_up, rwkv_a0, rwkv_a_up, rwkv_k_k, rwkv_k_a, rwkv_r_k, rwkv_ln_g, rwkv_ln_b, fox_q_g, fox_k_g, fox_f_b, gdn_conv, gdn_a_log, gdn_dt_bias, gdn_norm_g, diff_q_g, diff_k_g, diff_lq1, diff_lk1, diff_lq2, diff_lk2, diff_subln_g):
    bsz, seq, d = x.shape
    m = bsz * seq
    w_rwkv, w_fox, w_gdn, w_diff, w_z = _split_w_in(w_in)
    w_out_b = w_out.astype(BF16)
    x2d = x.reshape(m, d)
    for l in range(w_in.shape[0]):
        h = _rmsnorm(x2d, norm_g[l])
        seg = lambda w, tn: _matmul(h, w[l], tn, F32).reshape(bsz, seq, -1)
        p_rwkv, p_fox, p_gdn, p_diff = seg(w_rwkv, 640), seg(w_fox, 640), seg(w_gdn, 640), seg(w_diff, 512)
        sz = _matmul(h, w_z[l], 1024, BF16, silu=True, name="in_proj_gate")
        y_rwkv = _rwkv(p_rwkv, rwkv_mu[l], rwkv_w0[l], rwkv_w_up[l], rwkv_a0[l], rwkv_a_up[l], rwkv_k_k[l],
                       rwkv_k_a[l], rwkv_r_k[l].reshape(-1), rwkv_ln_g[l], rwkv_ln_b[l])
        y_fox = _fox(p_fox, fox_f_b[l], fox_q_g[l], fox_k_g[l])
        y_gdn = _gdn(p_gdn, gdn_conv[l], gdn_a_log[l], gdn_dt_bias[l], gdn_norm_g[l])
        y_diff = _diff(p_diff, l, diff_q_g[l], diff_k_g[l], diff_lq1[l], diff_lk1[l], diff_lq2[l], diff_lk2[l],
                       diff_subln_g[l])
        ys = [y.reshape(m, GROUP) for y in (y_rwkv, y_fox, y_gdn, y_diff)]
        x2d = _outproj(x2d, ys, sz, w_out_b[l])
    return x2d.reshape(bsz, seq, d)
```

```python
import functools
import math

import jax
import jax.numpy as jnp
from jax import lax
from jax.experimental import pallas as pl
from jax.experimental.pallas import tpu as pltpu

F32 = jnp.float32
BF16 = jnp.bfloat16
HI = lax.Precision.HIGHEST

D_MODEL = 2048
DEPTH = 4
GROUP = 1024
D_MIX = 4 * GROUP
LANES = 128
HEAD64 = 64
GDN_HEAD = 128
CHUNK = 64
LORA = 64
RWKV_GN_EPS = 64e-5
NORM_EPS = 1e-6
GDN_CONV = 4
N_RWKV = 3 * GROUP + 2 * LORA
N_FOX = 3 * GROUP + 16
N_GDN = 3 * GROUP + 16
N_DIFF = 3 * GROUP
SEG_PAD = 3200
VMEM_LIMIT = 56 * 1024 * 1024

NT = (((1,), (1,)), ((), ()))
TN = (((0,), (0,)), ((), ()))


def _params(n_axes):
    return pltpu.CompilerParams(dimension_semantics=("arbitrary",) * n_axes,
                                vmem_limit_bytes=VMEM_LIMIT)


def _softplus(x):
    return jnp.maximum(x, 0.0) + jnp.log1p(jnp.exp(-jnp.abs(x)))


def _iota(shape, axis):
    return lax.broadcasted_iota(jnp.int32, shape, axis)


def _each(fn, *lists):
    return [fn(*args) for args in zip(*lists)]


def _half_sum(x, low):
    s0 = jnp.sum(jnp.where(low, x, 0.0), axis=1, keepdims=True)
    s1 = jnp.sum(jnp.where(low, 0.0, x), axis=1, keepdims=True)
    return jnp.where(low, s0, s1)


def _mm(a, b):
    return jnp.dot(a.astype(BF16), b.astype(BF16), preferred_element_type=F32)


def _mm_nt(a, b):
    return lax.dot_general(a.astype(BF16), b.astype(BF16), NT, preferred_element_type=F32)


def _mm_tn(a, b):
    return lax.dot_general(a.astype(BF16), b.astype(BF16), TN, preferred_element_type=F32)


def _neumann_inverse(ns):
    eye = (_iota((LANES, LANES), 0) == _iota((LANES, LANES), 1)).astype(F32)
    ts = [eye + n for n in ns]
    ps = list(ns)
    for _ in range(5):
        ps = _each(_mm, ps, ps)
        ts = _each(lambda t, p: t + _mm(t, p), ts, ps)
    return ts


def _block_masks():
    ri = _iota((LANES, LANES), 0)
    ci = _iota((LANES, LANES), 1)
    same = (ri < CHUNK) == (ci < CHUNK)
    return same & (ri >= ci), same & (ri > ci), same & (ri <= ci), same


def _rmsnorm_body(x_ref, g_ref, o_ref):
    x = x_ref[...]
    ms = jnp.mean(x * x, axis=-1, keepdims=True)
    o_ref[...] = (x * lax.rsqrt(ms + NORM_EPS) * g_ref[...]).astype(o_ref.dtype)


def _rmsnorm(x2d, g, tm=512):
    m, d = x2d.shape
    return pl.pallas_call(
        _rmsnorm_body, grid=(m // tm,),
        in_specs=[pl.BlockSpec((tm, d), lambda i: (i, 0)), pl.BlockSpec((1, d), lambda i: (0, 0))],
        out_specs=pl.BlockSpec((tm, d), lambda i: (i, 0)),
        out_shape=jax.ShapeDtypeStruct((m, d), BF16), compiler_params=_params(1),
        name="rmsnorm")(x2d, g.reshape(1, d))


def _matmul_body(h_ref, w_ref, o_ref, *, silu):
    acc = jnp.dot(h_ref[...], w_ref[...], preferred_element_type=F32)
    if silu:
        acc = acc * jax.nn.sigmoid(acc)
    o_ref[...] = acc.astype(o_ref.dtype)


def _matmul(h, w, tn, out_dtype, silu=False, tm=1024, name="in_proj"):
    m, k = h.shape
    n = w.shape[1]
    tm = min(tm, m)
    return pl.pallas_call(
        functools.partial(_matmul_body, silu=silu), grid=(m // tm, n // tn),
        in_specs=[pl.BlockSpec((tm, k), lambda i, j: (i, 0)), pl.BlockSpec((k, tn), lambda i, j: (0, j))],
        out_specs=pl.BlockSpec((tm, tn), lambda i, j: (i, j)),
        out_shape=jax.ShapeDtypeStruct((m, n), out_dtype), compiler_params=_params(2),
        name=name)(h, w)


def _outproj_body(x_ref, yr_ref, yf_ref, yg_ref, yd_ref, sz_ref, w_ref, o_ref):
    acc = x_ref[...]
    for g, y_ref in enumerate((yr_ref, yf_ref, yg_ref, yd_ref)):
        gate = (y_ref[...].astype(F32) * sz_ref[:, g * GROUP:(g + 1) * GROUP].astype(F32)).astype(BF16)
        acc = acc + jnp.dot(gate, w_ref[g * GROUP:(g + 1) * GROUP, :], preferred_element_type=F32)
    o_ref[...] = acc


def _outproj(x2d, ys, sz, w_out, tm=512, tn=1024):
    m, d = x2d.shape
    tm = min(tm, m)
    yspec = pl.BlockSpec((tm, GROUP), lambda i, j: (i, 0))
    return pl.pallas_call(
        _outproj_body, grid=(m // tm, d // tn),
        in_specs=[pl.BlockSpec((tm, tn), lambda i, j: (i, j)), yspec, yspec, yspec, yspec,
                  pl.BlockSpec((tm, D_MIX), lambda i, j: (i, 0)),
                  pl.BlockSpec((D_MIX, tn), lambda i, j: (0, j))],
        out_specs=pl.BlockSpec((tm, tn), lambda i, j: (i, j)),
        out_shape=jax.ShapeDtypeStruct((m, d), F32), compiler_params=_params(2),
        name="out_proj")(x2d, *ys, sz, w_out)


def _aug_lanes(h):
    first = (1 - h) * HEAD64
    lane = _iota((1, LANES), 1)
    return lane == first, lane == first + 1


def _attend(qms, kn_ref, vb_ref, sigma_fns, qi, tq, tk):
    q_pos = qi * tq + _iota((tq, 1), 0)
    heads = range(len(qms))

    def scores(j):
        k0 = pl.multiple_of(j * tk, tk)
        return [lax.dot_general(qms[h], kn_ref[h, pl.ds(k0, tk), :], NT, preferred_element_type=F32) for h in heads]

    def step(j, s, carry, diagonal):
        k0 = pl.multiple_of(j * tk, tk)
        v_blk = vb_ref[pl.ds(k0, tk), :]
        if diagonal:
            visible = (k0 + _iota((1, tk), 1)) <= q_pos
            s = [jnp.where(visible, x, -jnp.inf) for x in s]
        sigma = [sigma_fns[h](j) for h in heads]
        m_new = [jnp.maximum(carry[h][0], jnp.max(s[h], axis=1, keepdims=True) + sigma[h]) for h in heads]
        alpha = [jnp.exp(carry[h][0] - m_new[h]) for h in heads]
        p = [jnp.exp(s[h] - (m_new[h] - sigma[h])).astype(BF16) for h in heads]
        pv = [jnp.dot(p[h], v_blk, preferred_element_type=F32) for h in heads]
        return tuple((m_new[h], alpha[h] * carry[h][1] + pv[h]) for h in heads)

    init = tuple((jnp.full((tq, 1), -jnp.inf, F32), jnp.zeros((tq, 2 * LANES), F32)) for _ in heads)
    n_full = (qi * tq) // tk

    def pipelined(j, sc):
        s_next = scores(j + 1)
        return s_next, step(j, sc[0], sc[1], False)

    s_last, carry = lax.fori_loop(0, n_full, pipelined, (scores(0), init))
    return [acc for _, acc in step(n_full, s_last, carry, True)]


def _half_rmsnorm(x, gain, low):
    ms = _half_sum(x * x, low) * (1.0 / HEAD64)
    return x * lax.rsqrt(ms + NORM_EPS) * gain


def _augment_queries(qn, low):
    out = []
    for h in range(2):
        a0, a1 = _aug_lanes(h)
        out.append(jnp.where(low if h == 0 else ~low, qn, jnp.where(a0 | a1, 1.0, 0.0)).astype(BF16))
    return out


def _stage_keys(k_ref, v_ref, kg_ref, kn_s, vb_s, low, seq, tk, key_bias):
    def body(i, c):
        r0 = pl.multiple_of(i * tk, tk)
        rows = pl.ds(r0, tk)
        kn = _half_rmsnorm(k_ref[0, rows, :], kg_ref[...], low)
        for h in range(2):
            d = key_bias(h, r0)
            d_hi = d.astype(BF16).astype(F32)
            a0, a1 = _aug_lanes(h)
            aug = jnp.where(a0, d_hi, jnp.where(a1, d - d_hi, 0.0))
            kn_s[h, rows, :] = jnp.where(low if h == 0 else ~low, kn, aug).astype(BF16)
        vb_s[rows, :LANES] = v_ref[0, rows, :].astype(BF16)
        vb_s[rows, LANES:] = jnp.ones((tk, LANES), BF16)
        return c

    lax.fori_loop(0, seq // tk, body, 0)


def _fox_body(q_ref, k_ref, v_ref, ck_ref, cq_ref, cc_ref, qg_ref, kg_ref, o_ref, kn_s, vb_s, *, seq, tq, tk):
    hp = pl.program_id(1)
    qi = pl.program_id(2)
    low = _iota((1, LANES), 1) < HEAD64

    def key_bias(h, r0):
        c_blk = jnp.sum(jnp.where(_iota((1, 16), 1) == 2 * hp + h, cc_ref[0, pl.ds(r0, tk), :], 0.0),
                        axis=1, keepdims=True)
        return c_blk[0:1, :] - c_blk

    @pl.when(qi == 0)
    def _():
        _stage_keys(k_ref, v_ref, kg_ref, kn_s, vb_s, low, seq, tk, key_bias)

    qn = _half_rmsnorm(q_ref[0], qg_ref[...], low) * (HEAD64 ** -0.5)
    sigma_fns = []
    for h in range(2):
        row = 2 * hp + h
        c_q0 = cq_ref[0, row, pl.ds(qi, 1), :][:, 0:1]
        sigma_fns.append(lambda j, row=row, c_q0=c_q0: c_q0 - ck_ref[0, row, pl.ds(j, 1), :][:, 0:1])
    acc0, acc1 = _attend(_augment_queries(qn, low), kn_s, vb_s, sigma_fns, qi, tq, tk)
    o_ref[0] = jnp.where(low, acc0[:, :LANES] / acc0[:, LANES:], acc1[:, :LANES] / acc1[:, LANES:]).astype(o_ref.dtype)


def _fox_cum_body(f_ref, fb_ref, c_ref, *, seq):
    x = f_ref[0] + fb_ref[...]
    logf = jnp.minimum(x, 0.0) - jnp.log1p(jnp.exp(-jnp.abs(x)))
    upper = (_iota((LANES, LANES), 0) <= _iota((LANES, LANES), 1)).astype(F32)
    carry = jnp.zeros((16, 1), F32)
    for blk in range(seq // LANES):
        sl = slice(blk * LANES, (blk + 1) * LANES)
        cb = jnp.dot(logf[:, sl], upper, precision=HI, preferred_element_type=F32) + carry
        c_ref[0, :, sl] = cb
        carry = cb[:, LANES - 1:LANES]


def _attention_scratch(seq):
    return [pltpu.VMEM((2, seq, LANES), BF16), pltpu.VMEM((seq, 2 * LANES), BF16)]


def _fox(p_fox, f_b, q_g, k_g, tq=256, tk=512):
    bsz, seq, _ = p_fox.shape
    tq, tk = min(tq, seq), min(tk, seq)
    f_t = jnp.transpose(p_fox[:, :, 3 * GROUP:3 * GROUP + 16], (0, 2, 1))
    c = pl.pallas_call(
        functools.partial(_fox_cum_body, seq=seq), grid=(bsz,),
        in_specs=[pl.BlockSpec((1, 16, seq), lambda b: (b, 0, 0)), pl.BlockSpec((16, 1), lambda b: (0, 0))],
        out_specs=pl.BlockSpec((1, 16, seq), lambda b: (b, 0, 0)),
        out_shape=jax.ShapeDtypeStruct((bsz, 16, seq), F32), compiler_params=_params(1),
        name="fox_cumsum")(f_t, f_b.reshape(16, 1))
    nb = GROUP // LANES
    gain = lambda g: jnp.tile(g, 2).reshape(1, LANES)
    return pl.pallas_call(
        functools.partial(_fox_body, seq=seq, tq=tq, tk=tk), grid=(bsz, nb, seq // tq),
        in_specs=[pl.BlockSpec((1, tq, LANES), lambda b, h, i: (b, i, h)),
                  pl.BlockSpec((1, seq, LANES), lambda b, h, i: (b, 0, nb + h)),
                  pl.BlockSpec((1, seq, LANES), lambda b, h, i: (b, 0, 2 * nb + h)),
                  pl.BlockSpec((1, 16, seq // tk, tk), lambda b, h, i: (b, 0, 0, 0)),
                  pl.BlockSpec((1, 16, seq // tq, tq), lambda b, h, i: (b, 0, 0, 0)),
                  pl.BlockSpec((1, seq, 16), lambda b, h, i: (b, 0, 0)),
                  pl.BlockSpec((1, LANES), lambda b, h, i: (0, 0)),
                  pl.BlockSpec((1, LANES), lambda b, h, i: (0, 0))],
        out_specs=pl.BlockSpec((1, tq, LANES), lambda b, h, i: (b, i, h)),
        out_shape=jax.ShapeDtypeStruct((bsz, seq, GROUP), BF16),
        scratch_shapes=_attention_scratch(seq),
        compiler_params=_params(3), name="fox_attention")(
            p_fox, p_fox, p_fox, c.reshape(bsz, 16, seq // tk, tk), c.reshape(bsz, 16, seq // tq, tq),
            jnp.transpose(c, (0, 2, 1)), gain(q_g), gain(k_g))


def _diff_body(q_ref, k_ref, v_ref, qg_ref, kg_ref, lam_ref, sg_ref, o_ref, kn_s, vb_s,
               *, seq, tq, tk, lam_init):
    head = pl.program_id(1)
    qi = pl.program_id(2)
    low = _iota((1, LANES), 1) < HEAD64
    slope = jnp.exp2(-(head + 1).astype(F32) * jnp.ones((1, 1), F32))

    @pl.when(qi == 0)
    def _():
        in_block = _iota((tk, 1), 0).astype(F32)
        _stage_keys(k_ref, v_ref, kg_ref, kn_s, vb_s, low, seq, tk, lambda h, r0: slope * in_block)

    lq1, lk1, lq2, lk2 = (lam_ref[i:i + 1, :] for i in range(4))
    lam = (jnp.exp(jnp.sum(lq1 * lk1, axis=1, keepdims=True))
           - jnp.exp(jnp.sum(lq2 * lk2, axis=1, keepdims=True)) + lam_init)

    def sigma(j):
        return slope * (j * tk - qi * tq).astype(F32)

    qn = _half_rmsnorm(q_ref[0], qg_ref[...], low) * (HEAD64 ** -0.5)
    acc0, acc1 = _attend(_augment_queries(qn, low), kn_s, vb_s, [sigma, sigma], qi, tq, tk)
    o = acc0[:, :LANES] / acc0[:, LANES:] - lam * (acc1[:, :LANES] / acc1[:, LANES:])
    ms = jnp.mean(o * o, axis=1, keepdims=True)
    o_ref[0] = (o * lax.rsqrt(ms + 1e-5) * sg_ref[...] * (1.0 - lam_init)).astype(o_ref.dtype)


def _diff(p_diff, layer, q_g, k_g, lq1, lk1, lq2, lk2, subln_g, tq=256, tk=512):
    bsz, seq, _ = p_diff.shape
    tq, tk = min(tq, seq), min(tk, seq)
    nb = GROUP // LANES
    lam_init = 0.8 - 0.6 * math.exp(-0.3 * layer)
    gain = lambda g: jnp.tile(g, 2).reshape(1, LANES)
    lam_rows = jnp.stack([lq1, lk1, lq2, lk2])
    return pl.pallas_call(
        functools.partial(_diff_body, seq=seq, tq=tq, tk=tk, lam_init=lam_init), grid=(bsz, nb, seq // tq),
        in_specs=[pl.BlockSpec((1, tq, LANES), lambda b, h, i: (b, i, h)),
                  pl.BlockSpec((1, seq, LANES), lambda b, h, i: (b, 0, nb + h)),
                  pl.BlockSpec((1, seq, LANES), lambda b, h, i: (b, 0, 2 * nb + h)),
                  pl.BlockSpec((1, LANES), lambda b, h, i: (0, 0)),
                  pl.BlockSpec((1, LANES), lambda b, h, i: (0, 0)),
                  pl.BlockSpec((4, HEAD64), lambda b, h, i: (0, 0)),
                  pl.BlockSpec((1, LANES), lambda b, h, i: (0, 0))],
        out_specs=pl.BlockSpec((1, tq, LANES), lambda b, h, i: (b, i, h)),
        out_shape=jax.ShapeDtypeStruct((bsz, seq, GROUP), BF16),
        scratch_shapes=_attention_scratch(seq),
        compiler_params=_params(3), name="diff_attention")(
            p_diff, p_diff, p_diff, gain(q_g), gain(k_g), lam_rows, subln_g.reshape(1, LANES))


def _gdn_body(q_ref, k_ref, v_ref, qh_ref, kh_ref, vh_ref, gate_ref, cwq_ref, cwk_ref, cwv_ref, alog_ref, dt_ref,
              ng_ref, o_ref, qw_s, au_s, p_s, q_s, eg_s, state_s, *, ts, nheads, unroll):
    pair = 2 * CHUNK
    first_tile = pl.program_id(2) == 0
    lane = _iota((1, LANES), 1)
    pick = lambda x, idx: jnp.sum(jnp.where(lane == idx, x, 0.0), axis=1, keepdims=True)
    m_tril, m_strict, m_upper, m_same = _block_masks()
    eye = _iota((pair, pair), 0) == _iota((pair, pair), 1)
    head_ids = [pl.program_id(1) * nheads + hd for hd in range(nheads)]
    a_scale = [-jnp.exp(pick(alog_ref[...], h)) for h in head_ids]
    dt_bias = [pick(dt_ref[...], h) for h in head_ids]

    @pl.when(first_tile)
    def _():
        state_s[...] = jnp.zeros_like(state_s)

    def prepare(it, carry):
        chains = [(hd, it * unroll + u) for u in range(unroll) for hd in range(nheads)]
        lanes = [slice(hd * LANES, (hd + 1) * LANES) for hd, _ in chains]
        r0s = [pl.multiple_of(cp * pair, pair) for _, cp in chains]

        def conv_silu(ref, halo_ref, cw_ref):
            def one(ln, r0, chain):
                cur = ref[0, pl.ds(r0, pair), ln]
                inside = ref[0, pl.ds(pl.multiple_of(jnp.maximum(r0 - 8, 0), 8), 8), ln]
                before = jnp.where(first_tile, 0.0, halo_ref[0, :, ln])
                ext = jnp.concatenate([jnp.where(chain[1] == 0, before, inside), cur], axis=0)
                acc = cur * cw_ref[GDN_CONV - 1:GDN_CONV, ln]
                for j in range(1, GDN_CONV):
                    acc = acc + pltpu.roll(ext, j, 0)[8:, :] * cw_ref[GDN_CONV - 1 - j:GDN_CONV - j, ln]
                return acc * jax.nn.sigmoid(acc)
            return _each(one, lanes, r0s, chains)

        q = conv_silu(q_ref, qh_ref, cwq_ref)
        k = conv_silu(k_ref, kh_ref, cwk_ref)
        v = conv_silu(v_ref, vh_ref, cwv_ref)
        q = _each(lambda x: x * lax.rsqrt(jnp.sum(x * x, axis=1, keepdims=True) + 1e-6) * (GDN_HEAD ** -0.5), q)
        k = _each(lambda x: x * lax.rsqrt(jnp.sum(x * x, axis=1, keepdims=True) + 1e-6), k)
        gates = [gate_ref[0, pl.ds(r0, pair), :] for r0 in r0s]
        g_col = [a_scale[hd] * _softplus(pick(g, head_ids[hd]) + dt_bias[hd]) for (hd, _), g in zip(chains, gates)]
        beta = [jax.nn.sigmoid(pick(g, head_ids[hd] + 8)) for (hd, _), g in zip(chains, gates)]
        g_row = _each(lambda g: jnp.sum(jnp.where(eye, g, 0.0), axis=0, keepdims=True), g_col)
        gc_col = _each(lambda g: jnp.sum(jnp.where(m_tril, g, 0.0), axis=1, keepdims=True), g_row)
        gc_row = _each(lambda g: jnp.sum(jnp.where(m_upper, g, 0.0), axis=0, keepdims=True), g_col)
        g_last = _each(lambda g: jnp.sum(jnp.where(m_same, g, 0.0), axis=1, keepdims=True), g_row)
        decay = _each(lambda c, r: jnp.exp(jnp.where(m_tril, c - r, -jnp.inf)), gc_col, gc_row)
        exp_gc = _each(jnp.exp, gc_col)
        kb = _each(lambda x, b: x * b, k, beta)
        kk = _each(_mm_nt, kb, k)
        t = _neumann_inverse(_each(lambda x, d: -jnp.where(m_strict, x * d, 0.0), kk, decay))
        uw = _each(lambda t_, v_, b, kb_, e: _mm(t_, jnp.concatenate([v_ * b, kb_ * e], axis=1)),
                   t, v, beta, kb, exp_gc)
        ai = _each(lambda q_, k_, d: _mm_nt(q_, k_) * d, q, k, decay)
        aiuw = _each(_mm, ai, uw)
        kd = _each(lambda k_, gl, gc: k_ * jnp.exp(gl - gc), k, g_last, gc_col)
        kuw = [[_mm_tn(kd_[half * CHUNK:(half + 1) * CHUNK], uw_[half * CHUNK:(half + 1) * CHUNK])
                for kd_, uw_ in zip(kd, uw)] for half in range(2)]
        for i, (hd, cp) in enumerate(chains):
            rows = pl.ds(r0s[i], pair)
            qw_s[hd, rows, :] = (q[i] * exp_gc[i] - aiuw[i][:, LANES:]).astype(BF16)
            au_s[hd, rows, :] = aiuw[i][:, :LANES]
            e_last = jnp.exp(g_last[i])
            for half in range(2):
                c = 2 * cp + half
                sq = pl.ds(pl.multiple_of(c * LANES, LANES), LANES)
                p_s[hd, sq, :] = (-kuw[half][i][:, LANES:]).astype(BF16)
                q_s[hd, sq, :] = kuw[half][i][:, :LANES]
                eg_s[hd, pl.ds(c, 1), :] = jnp.broadcast_to(e_last[half * CHUNK:half * CHUNK + 1], (1, LANES))
        return carry

    lax.fori_loop(0, ts // pair // unroll, prepare, 0)

    def recur(c, states):
        rows = pl.ds(pl.multiple_of(c * CHUNK, CHUNK), CHUNK)
        sq = pl.ds(pl.multiple_of(c * LANES, LANES), LANES)
        sb = [s.astype(BF16) for s in states]
        o = [jnp.dot(qw_s[hd, rows, :], sb[hd], preferred_element_type=F32) + au_s[hd, rows, :]
             for hd in range(nheads)]
        states = tuple(states[hd] * eg_s[hd, pl.ds(c, 1), :]
                       + jnp.dot(p_s[hd, sq, :], sb[hd], preferred_element_type=F32) + q_s[hd, sq, :]
                       for hd in range(nheads))
        for hd in range(nheads):
            ms = jnp.mean(o[hd] * o[hd], axis=1, keepdims=True)
            o_ref[0, rows, hd * LANES:(hd + 1) * LANES] = (
                o[hd] * lax.rsqrt(ms + NORM_EPS) * ng_ref[...]).astype(o_ref.dtype)
        return states

    final = lax.fori_loop(0, ts // CHUNK, recur, tuple(state_s[hd] for hd in range(nheads)))
    for hd in range(nheads):
        state_s[hd] = final[hd]


def _gdn(p_gdn, conv_w, a_log, dt_bias, norm_g, ts=1024, nheads=2, unroll=2):
    bsz, seq, _ = p_gdn.shape
    ts = min(ts, seq)
    nb = GROUP // LANES // nheads
    width = nheads * LANES
    n_chunks = ts // CHUNK
    pad = lambda v: jnp.pad(v, (0, LANES - v.shape[0])).reshape(1, LANES)
    seq_spec = lambda off: pl.BlockSpec((1, ts, width), lambda b, h, s: (b, s, off + h))
    halo_spec = lambda off: pl.BlockSpec((1, 8, width), lambda b, h, s: (b, jnp.maximum(s * (ts // 8) - 1, 0), off + h))
    cw_spec = lambda off: pl.BlockSpec((GDN_CONV, width), lambda b, h, s: (0, off + h))
    row_spec = pl.BlockSpec((1, LANES), lambda b, h, s: (0, 0))
    return pl.pallas_call(
        functools.partial(_gdn_body, ts=ts, nheads=nheads, unroll=unroll), grid=(bsz, nb, seq // ts),
        in_specs=[seq_spec(0), seq_spec(nb), seq_spec(2 * nb), halo_spec(0), halo_spec(nb), halo_spec(2 * nb),
                  pl.BlockSpec((1, ts, LANES), lambda b, h, s: (b, s, 3 * GROUP // LANES)),
                  cw_spec(0), cw_spec(nb), cw_spec(2 * nb), row_spec, row_spec, row_spec],
        out_specs=pl.BlockSpec((1, ts, width), lambda b, h, s: (b, s, h)),
        out_shape=jax.ShapeDtypeStruct((bsz, seq, GROUP), BF16),
        scratch_shapes=[pltpu.VMEM((nheads, ts, LANES), BF16), pltpu.VMEM((nheads, ts, LANES), F32),
                        pltpu.VMEM((nheads, n_chunks * LANES, LANES), BF16),
                        pltpu.VMEM((nheads, n_chunks * LANES, LANES), F32),
                        pltpu.VMEM((nheads, max(n_chunks, 8), LANES), F32),
                        pltpu.VMEM((nheads, GDN_HEAD, GDN_HEAD), F32)],
        compiler_params=_params(3), name="gated_deltanet")(
            p_gdn, p_gdn, p_gdn, p_gdn, p_gdn, p_gdn, p_gdn, conv_w, conv_w, conv_w, pad(a_log), pad(dt_bias),
            norm_g.reshape(1, LANES))


def _rwkv_prep_body(p_ref, halo_ref, mu_ref, w0_ref, wup_ref, a0_ref, aup_ref, kk_ref, ka_ref,
                    r_ref, lw_ref, k_ref, v_ref, kkn_ref, kka_ref):
    p = p_ref[0]
    first = pl.program_id(1) == 0
    last_prev = jnp.where(first, 0.0, halo_ref[0, 7:8, :])
    prev = jnp.where(_iota((p.shape[0], 1), 0) == 0, last_prev, pltpu.roll(p, 1, 0))
    xs = p + mu_ref[...] * (prev - p)
    r, k, v = xs[:, :GROUP], xs[:, GROUP:2 * GROUP], xs[:, 2 * GROUP:3 * GROUP]
    lora = xs[:, 3 * GROUP:]
    w = w0_ref[...] + jnp.dot(jnp.tanh(lora).astype(BF16), wup_ref[...], preferred_element_type=F32)
    a = jax.nn.sigmoid(a0_ref[...] + jnp.dot(lora.astype(BF16), aup_ref[...], preferred_element_type=F32))
    low = _iota((1, LANES), 1) < HEAD64
    r_ref[0] = r
    v_ref[0] = v
    lw_ref[0] = -jnp.exp(-_softplus(-w) - 0.5)
    k_ref[0] = k * (1.0 + (a - 1.0) * ka_ref[...])
    for blk in range(GROUP // LANES):
        sl = slice(blk * LANES, (blk + 1) * LANES)
        kk0 = k[:, sl] * kk_ref[:, sl]
        kkn = kk0 * lax.rsqrt(_half_sum(kk0 * kk0, low) + 1e-6)
        kkn_ref[0, :, sl] = kkn
        kka_ref[0, :, sl] = kkn * a[:, sl]


def _rwkv_body(r_ref, lw_ref, k_ref, v_ref, kk_ref, kka_ref, rk_ref, lng_ref, lnb_ref, o_ref,
               ra_s, y1_s, p_s, q_s, gam_s, state_s, *, ts, npairs, unroll):
    low = _iota((1, LANES), 1) < HEAD64
    m_tril, m_strict, _, _ = _block_masks()
    tril_b = (_iota((CHUNK, CHUNK), 0) >= _iota((CHUNK, CHUNK), 1)).astype(BF16)
    split = lambda x: jnp.concatenate([jnp.where(low, x, 0.0), jnp.where(low, 0.0, x)], axis=0)
    unsplit = lambda x: x[:CHUNK] + x[CHUNK:]
    mul = lambda a, b: a * b

    @pl.when(pl.program_id(2) == 0)
    def _():
        state_s[...] = jnp.zeros_like(state_s)

    def prepare(it, carry):
        chains = [(pr, it * unroll + u) for u in range(unroll) for pr in range(npairs)]
        rows = [pl.ds(pl.multiple_of(c * CHUNK, CHUNK), CHUNK) for _, c in chains]
        lanes = [slice(pr * LANES, (pr + 1) * LANES) for pr, _ in chains]
        ld = lambda ref: [ref[0, rw, ln] for rw, ln in zip(rows, lanes)]
        lw = ld(lw_ref)
        lw_hi = _each(lambda x: x.astype(BF16), lw)
        lw_lo = _each(lambda x, hi: (x - hi.astype(F32)).astype(BF16), lw, lw_hi)
        cum = _each(lambda hi, lo: (jnp.dot(tril_b, hi, preferred_element_type=F32)
                                    + jnp.dot(tril_b, lo, preferred_element_type=F32)), lw_hi, lw_lo)
        gam = _each(jnp.exp, cum)
        inv = _each(lambda c: jnp.exp(-c), cum)
        g_end = _each(lambda g: g[CHUNK - 1:CHUNK, :], gam)
        at = _each(lambda kk, c, w: -kk * jnp.exp(c - w), ld(kk_ref), cum, lw)
        rt = _each(mul, ld(r_ref), gam)
        bt = _each(mul, ld(kka_ref), inv)
        kt = _each(mul, ld(k_ref), inv)
        a_st = _each(split, at)
        v_st = _each(split, ld(v_ref))
        gram = _each(lambda a, r, b, k: _mm_nt(jnp.concatenate([a, split(r)], axis=0),
                                               jnp.concatenate([b, b, k, k], axis=0)), a_st, rt, bt, kt)
        n_ab = _each(lambda g: jnp.where(m_strict, g[:LANES, :LANES], 0.0), gram)
        n_ak = _each(lambda g: jnp.where(m_strict, g[:LANES, LANES:], 0.0), gram)
        n_rb = _each(lambda g: jnp.where(m_tril, g[LANES:, :LANES], 0.0), gram)
        n_rk = _each(lambda g: jnp.where(m_tril, g[LANES:, LANES:], 0.0), gram)
        t = _neumann_inverse(n_ab)
        akv = _each(_mm, n_ak, v_st)
        z = _each(lambda t_, a, x: _mm(t_, jnp.concatenate([a, x], axis=1)), t, a_st, akv)
        rz = _each(_mm, n_rb, z)
        rkv = _each(_mm, n_rk, v_st)
        b_st = _each(lambda b, g: split(b * g), bt, g_end)
        k_st = _each(lambda k, g: split(k * g), kt, g_end)
        pm = _each(lambda b, z_: _mm_tn(b, z_[:, :LANES]), b_st, z)
        qm = _each(lambda b, k, z_, v: _mm_tn(jnp.concatenate([b, k], axis=0),
                                              jnp.concatenate([z_[:, LANES:], v], axis=0)), b_st, k_st, z, v_st)
        for i, (pr, c) in enumerate(chains):
            sq = pl.ds(pl.multiple_of(c * LANES, LANES), LANES)
            ra_s[pr, rows[i], :] = (rt[i] + unsplit(rz[i][:, :LANES])).astype(BF16)
            y1_s[pr, rows[i], :] = unsplit(rkv[i] + rz[i][:, LANES:])
            p_s[pr, sq, :] = pm[i].astype(BF16)
            q_s[pr, sq, :] = qm[i]
            gam_s[pr, sq, :] = jnp.broadcast_to(g_end[i], (LANES, LANES)).T
        return carry

    lax.fori_loop(0, ts // CHUNK // unroll, prepare, 0)

    def recur(c, hs):
        rows = pl.ds(pl.multiple_of(c * CHUNK, CHUNK), CHUNK)
        sq = pl.ds(pl.multiple_of(c * LANES, LANES), LANES)
        hb = [h.astype(BF16) for h in hs]
        y = [jnp.dot(ra_s[pr, rows, :], hb[pr], preferred_element_type=F32) + y1_s[pr, rows, :]
             for pr in range(npairs)]
        hs = tuple(gam_s[pr, sq, :] * hs[pr] + jnp.dot(p_s[pr, sq, :], hb[pr], preferred_element_type=F32)
                   + q_s[pr, sq, :] for pr in range(npairs))
        for pr in range(npairs):
            ln = slice(pr * LANES, (pr + 1) * LANES)
            mean = _half_sum(y[pr], low) * (1.0 / HEAD64)
            yc = y[pr] - mean
            var = _half_sum(yc * yc, low) * (1.0 / HEAD64)
            rkk = r_ref[0, rows, ln] * k_ref[0, rows, ln] * rk_ref[:, ln]
            out = (yc * lax.rsqrt(var + RWKV_GN_EPS) * lng_ref[:, ln] + lnb_ref[:, ln]
                   + _half_sum(rkk, low) * v_ref[0, rows, ln])
            o_ref[0, rows, ln] = out.astype(o_ref.dtype)
        return hs

    final = lax.fori_loop(0, ts // CHUNK, recur, tuple(state_s[pr] for pr in range(npairs)))
    for pr in range(npairs):
        state_s[pr] = final[pr]


def _rwkv(p_rwkv, mu, w0, w_up, a0, a_up, k_k, k_a, r_k, ln_g, ln_b, tm=256, ts=1024, npairs=2, unroll=4):
    bsz, seq, _ = p_rwkv.shape
    tm, ts = min(tm, seq), min(ts, seq)
    unroll = min(unroll, ts // CHUNK)
    width = npairs * LANES
    nb = GROUP // width
    zeros = jnp.zeros((LORA, GROUP), F32)
    wup = jnp.concatenate([w_up, zeros], axis=0).astype(BF16)
    aup = jnp.concatenate([zeros, a_up], axis=0).astype(BF16)
    row = lambda v: v.reshape(1, -1)
    full = lambda n: pl.BlockSpec((1, n), lambda b, i: (0, 0))
    out_spec = pl.BlockSpec((1, tm, GROUP), lambda b, i: (b, i, 0))
    seq_f32 = jax.ShapeDtypeStruct((bsz, seq, GROUP), F32)
    r, lw, k, v, kk, kka = pl.pallas_call(
        _rwkv_prep_body, grid=(bsz, seq // tm),
        in_specs=[pl.BlockSpec((1, tm, N_RWKV), lambda b, i: (b, i, 0)),
                  pl.BlockSpec((1, 8, N_RWKV), lambda b, i: (b, jnp.maximum(i * (tm // 8) - 1, 0), 0)),
                  full(N_RWKV), full(GROUP),
                  pl.BlockSpec((2 * LORA, GROUP), lambda b, i: (0, 0)), full(GROUP),
                  pl.BlockSpec((2 * LORA, GROUP), lambda b, i: (0, 0)), full(GROUP), full(GROUP)],
        out_specs=[out_spec] * 6, out_shape=[seq_f32] * 6, compiler_params=_params(2),
        name="rwkv_prep")(p_rwkv, p_rwkv, row(mu), row(w0), wup, row(a0), aup, row(k_k), row(k_a))
    n_chunks = ts // CHUNK
    seq_spec = pl.BlockSpec((1, ts, width), lambda b, h, s: (b, s, h))
    par_spec = pl.BlockSpec((1, width), lambda b, h, s: (0, h))
    return pl.pallas_call(
        functools.partial(_rwkv_body, ts=ts, npairs=npairs, unroll=unroll), grid=(bsz, nb, seq // ts),
        in_specs=[seq_spec] * 6 + [par_spec] * 3,
        out_specs=seq_spec, out_shape=jax.ShapeDtypeStruct((bsz, seq, GROUP), BF16),
        scratch_shapes=[pltpu.VMEM((npairs, ts, LANES), BF16), pltpu.VMEM((npairs, ts, LANES), F32),
                        pltpu.VMEM((npairs, n_chunks * LANES, LANES), BF16),
                        pltpu.VMEM((npairs, n_chunks * LANES, LANES), F32),
                        pltpu.VMEM((npairs, n_chunks * LANES, LANES), F32),
                        pltpu.VMEM((npairs, LANES, LANES), F32)],
        compiler_params=_params(3), name="rwkv7")(
            r, lw, k, v, kk, kka, row(r_k), row(ln_g), row(ln_b))


def _split_w_in(w_in):
    b0, b1, b2, b3 = N_RWKV, N_RWKV + N_FOX, N_RWKV + N_FOX + N_GDN, N_RWKV + N_FOX + N_GDN + N_DIFF
    pad = lambda w: jnp.pad(w, ((0, 0), (0, 0), (0, SEG_PAD - w.shape[-1])))
    wb = w_in.astype(BF16)
    return wb[..., :b0], pad(wb[..., b0:b1]), pad(wb[..., b1:b2]), wb[..., b2:b3], wb[..., b3:]


def kernel(x, norm_g, w_in, w_out, rwkv_mu, rwkv_w0, rwkv_w_up, rwkv_a0, rwkv_a_up, rwkv_k_k, rwkv_k_a, rwkv_r_k, rwkv_ln_g, rwkv_ln_b, fox_q_g, fox_k_g, fox_f_b, gdn_conv, gdn_a_log, gdn_dt_bias, gdn_norm_g, diff_q_g, diff_k_g, diff_lq1, diff_lk1, diff_lq2, diff_lk2, diff_subln_g):
    bsz, seq, d = x.shape
    m = bsz * seq
    w_rwkv, w_fox, w_gdn, w_diff, w_z = _split_w_in(w_in)
    w_out_b = w_out.astype(BF16)
    x2d = x.reshape(m, d)
    for l in range(w_in.shape[0]):
        h = _rmsnorm(x2d, norm_g[l])
        seg = lambda w, tn: _matmul(h, w[l], tn, F32).reshape(bsz, seq, -1)
        p_rwkv, p_fox, p_gdn, p_diff = seg(w_rwkv, 640), seg(w_fox, 640), seg(w_gdn, 640), seg(w_diff, 512)
        sz = _matmul(h, w_z[l], 1024, BF16, silu=True, name="in_proj_gate")
        y_rwkv = _rwkv(p_rwkv, rwkv_mu[l], rwkv_w0[l], rwkv_w_up[l], rwkv_a0[l], rwkv_a_up[l], rwkv_k_k[l],
                       rwkv_k_a[l], rwkv_r_k[l].reshape(-1), rwkv_ln_g[l], rwkv_ln_b[l])
        y_fox = _fox(p_fox, fox_f_b[l], fox_q_g[l], fox_k_g[l])
        y_gdn = _gdn(p_gdn, gdn_conv[l], gdn_a_log[l], gdn_dt_bias[l], gdn_norm_g[l])
        y_diff = _diff(p_diff, l, diff_q_g[l], diff_k_g[l], diff_lq1[l], diff_lk1[l], diff_lq2[l], diff_lk2[l],
                       diff_subln_g[l])
        ys = [y.reshape(m, GROUP) for y in (y_rwkv, y_fox, y_gdn, y_diff)]
        x2d = _outproj(x2d, ys, sz, w_out_b[l])
    return x2d.reshape(bsz, seq, d)
```

```python
import functools
import math

import jax
import jax.numpy as jnp
from jax import lax
from jax.experimental import pallas as pl
from jax.experimental.pallas import tpu as pltpu

F32 = jnp.float32
BF16 = jnp.bfloat16
HI = lax.Precision.HIGHEST

D_MODEL = 2048
DEPTH = 4
GROUP = 1024
D_MIX = 4 * GROUP
LANES = 128
HEAD64 = 64
GDN_HEAD = 128
CHUNK = 64
LORA = 64
RWKV_GN_EPS = 64e-5
NORM_EPS = 1e-6
GDN_CONV = 4
N_RWKV = 3 * GROUP + 2 * LORA
N_FOX = 3 * GROUP + 16
N_GDN = 3 * GROUP + 16
N_DIFF = 3 * GROUP
SEG_PAD = 3200
GDN_PAD = 3328
IN_PROJ_TN = 1280
VMEM_LIMIT = 56 * 1024 * 1024

NT = (((1,), (1,)), ((), ()))
TN = (((0,), (0,)), ((), ()))


def _params(n_axes):
    return pltpu.CompilerParams(dimension_semantics=("arbitrary",) * n_axes,
                                vmem_limit_bytes=VMEM_LIMIT)


def _softplus(x):
    return jnp.maximum(x, 0.0) + jnp.log1p(jnp.exp(-jnp.abs(x)))


def _iota(shape, axis):
    return lax.broadcasted_iota(jnp.int32, shape, axis)


def _each(fn, *lists):
    return [fn(*args) for args in zip(*lists)]


def _half_sum(x, low):
    s0 = jnp.sum(jnp.where(low, x, 0.0), axis=1, keepdims=True)
    s1 = jnp.sum(jnp.where(low, 0.0, x), axis=1, keepdims=True)
    return jnp.where(low, s0, s1)


def _mm(a, b):
    return jnp.dot(a.astype(BF16), b.astype(BF16), preferred_element_type=F32)


def _mm_nt(a, b):
    return lax.dot_general(a.astype(BF16), b.astype(BF16), NT, preferred_element_type=F32)


def _mm_tn(a, b):
    return lax.dot_general(a.astype(BF16), b.astype(BF16), TN, preferred_element_type=F32)


def _neumann_inverse(ns):
    eye = (_iota((LANES, LANES), 0) == _iota((LANES, LANES), 1)).astype(F32)
    ts = [eye + n for n in ns]
    ps = list(ns)
    for _ in range(5):
        ps = _each(_mm, ps, ps)
        ts = _each(lambda t, p: t + _mm(t, p), ts, ps)
    return ts


def _block_masks():
    ri = _iota((LANES, LANES), 0)
    ci = _iota((LANES, LANES), 1)
    same = (ri < CHUNK) == (ci < CHUNK)
    return same & (ri >= ci), same & (ri > ci), same & (ri <= ci), same


def _rmsnorm_body(x_ref, g_ref, o_ref):
    x = x_ref[...]
    ms = jnp.mean(x * x, axis=-1, keepdims=True)
    o_ref[...] = (x * lax.rsqrt(ms + NORM_EPS) * g_ref[...]).astype(o_ref.dtype)


def _rmsnorm(x2d, g, tm=512):
    m, d = x2d.shape
    return pl.pallas_call(
        _rmsnorm_body, grid=(m // tm,),
        in_specs=[pl.BlockSpec((tm, d), lambda i: (i, 0)), pl.BlockSpec((1, d), lambda i: (0, 0))],
        out_specs=pl.BlockSpec((tm, d), lambda i: (i, 0)),
        out_shape=jax.ShapeDtypeStruct((m, d), BF16), compiler_params=_params(1),
        name="rmsnorm")(x2d, g.reshape(1, d))


def _matmul_body(h_ref, w_ref, o_ref, *, silu):
    acc = jnp.dot(h_ref[...], w_ref[...], preferred_element_type=F32)
    if silu:
        acc = acc * jax.nn.sigmoid(acc)
    o_ref[...] = acc.astype(o_ref.dtype)


def _matmul(h, w, tn, out_dtype, silu=False, tm=1024, name="in_proj"):
    m, k = h.shape
    n = w.shape[1]
    tm = min(tm, m)
    return pl.pallas_call(
        functools.partial(_matmul_body, silu=silu), grid=(m // tm, n // tn),
        in_specs=[pl.BlockSpec((tm, k), lambda i, j: (i, 0)), pl.BlockSpec((k, tn), lambda i, j: (0, j))],
        out_specs=pl.BlockSpec((tm, tn), lambda i, j: (i, j)),
        out_shape=jax.ShapeDtypeStruct((m, n), out_dtype), compiler_params=_params(2),
        name=name)(h, w)


def _outproj_body(x_ref, yr_ref, yf_ref, yg_ref, yd_ref, sz_ref, w_ref, o_ref):
    acc = x_ref[...]
    for g, y_ref in enumerate((yr_ref, yf_ref, yg_ref, yd_ref)):
        gate = (y_ref[...].astype(F32) * sz_ref[:, g * GROUP:(g + 1) * GROUP].astype(F32)).astype(BF16)
        acc = acc + jnp.dot(gate, w_ref[g * GROUP:(g + 1) * GROUP, :], preferred_element_type=F32)
    o_ref[...] = acc


def _outproj(x2d, ys, sz, w_out, tm=512, tn=1024):
    m, d = x2d.shape
    tm = min(tm, m)
    yspec = pl.BlockSpec((tm, GROUP), lambda i, j: (i, 0))
    return pl.pallas_call(
        _outproj_body, grid=(m // tm, d // tn),
        in_specs=[pl.BlockSpec((tm, tn), lambda i, j: (i, j)), yspec, yspec, yspec, yspec,
                  pl.BlockSpec((tm, D_MIX), lambda i, j: (i, 0)),
                  pl.BlockSpec((D_MIX, tn), lambda i, j: (0, j))],
        out_specs=pl.BlockSpec((tm, tn), lambda i, j: (i, j)),
        out_shape=jax.ShapeDtypeStruct((m, d), F32), compiler_params=_params(2),
        name="out_proj")(x2d, *ys, sz, w_out)


def _aug_lanes(h):
    first = (1 - h) * HEAD64
    lane = _iota((1, LANES), 1)
    return lane == first, lane == first + 1


VT_ROWS = LANES + 16


def _attend(qts, kn_ref, vt_ref, sa_ref, sb_ref, acc_ref, sigma_fns, qi, tq, tk):
    q_pos = qi * tq + _iota((1, tq), 1)
    heads = range(len(qts))

    def produce(j, s_ref, diagonal=False):
        k0 = pl.multiple_of(j * tk, tk)
        s = [jnp.dot(kn_ref[h, pl.ds(k0, tk), :], qts[h], preferred_element_type=F32) for h in heads]
        if diagonal:
            visible = (k0 + _iota((tk, 1), 0)) <= q_pos
            s = [jnp.where(visible, x, -jnp.inf) for x in s]
        for h in heads:
            s_ref[h] = s[h]
        return [jnp.max(x, axis=0, keepdims=True) for x in s]

    def consume(j, s_ref, s_max, m):
        sigma = [sigma_fns[h](j) for h in heads]
        m_new = [jnp.maximum(m[h], s_max[h] + sigma[h]) for h in heads]
        alpha = [jnp.exp(m[h] - m_new[h]) for h in heads]
        p = [jnp.exp(s_ref[h] - (m_new[h] - sigma[h])).astype(BF16) for h in heads]
        pv = [jnp.dot(vt_ref[j], p[h], preferred_element_type=F32) for h in heads]
        for h in heads:
            acc_ref[h] = alpha[h] * acc_ref[h] + pv[h]
        return m_new

    n_full = (qi * tq) // tk
    acc_ref[...] = jnp.zeros_like(acc_ref)
    max_a = produce(n_full, sa_ref, diagonal=True)

    def pair(t, carry):
        m, max_a, in_a = carry
        max_b = produce(2 * t, sb_ref)
        m = consume(in_a, sa_ref, max_a, m)
        max_a = produce(2 * t + 1, sa_ref)
        m = consume(2 * t, sb_ref, max_b, m)
        return m, max_a, 2 * t + 1

    m, max_a, in_a = lax.fori_loop(0, n_full // 2, pair,
                                   ([jnp.full((1, tq), -jnp.inf, F32)] * 2, max_a, n_full))

    @pl.when(n_full % 2 == 1)
    def _():
        max_b = produce(n_full - 1, sb_ref)
        consume(n_full - 1, sb_ref, max_b, consume(in_a, sa_ref, max_a, m))

    @pl.when(n_full % 2 == 0)
    def _():
        consume(in_a, sa_ref, max_a, m)


def _half_rmsnorm(x, gain, low):
    ms = _half_sum(x * x, low) * (1.0 / HEAD64)
    return x * lax.rsqrt(ms + NORM_EPS) * gain


def _augment_queries(qn):
    qt = qn.T
    row = _iota((LANES, 1), 0)
    out = []
    for h in range(2):
        first = (1 - h) * HEAD64
        own = row < HEAD64 if h == 0 else row >= HEAD64
        out.append(jnp.where(own, qt, jnp.where((row == first) | (row == first + 1), 1.0, 0.0)).astype(BF16))
    return out


def _normalised_output(acc):
    return acc[:LANES] / acc[LANES:LANES + 1]


def _stage_keys(k_ref, v_ref, kg_ref, kn_s, vt_s, low, seq, tk, key_bias):
    def body(i, c):
        r0 = pl.multiple_of(i * tk, tk)
        rows = pl.ds(r0, tk)
        kn = _half_rmsnorm(k_ref[0, rows, :], kg_ref[...], low)
        for h in range(2):
            d = key_bias(h, r0)
            d_hi = d.astype(BF16).astype(F32)
            a0, a1 = _aug_lanes(h)
            aug = jnp.where(a0, d_hi, jnp.where(a1, d - d_hi, 0.0))
            kn_s[h, rows, :] = jnp.where(low if h == 0 else ~low, kn, aug).astype(BF16)
        vt_s[i, :LANES, :] = v_ref[0, rows, :].T.astype(BF16)
        vt_s[i, LANES:, :] = jnp.ones((VT_ROWS - LANES, tk), BF16)
        return c

    lax.fori_loop(0, seq // tk, body, 0)


def _fox_body(q_ref, k_ref, v_ref, ck_ref, cq_ref, cc_ref, qg_ref, kg_ref, o_ref, kn_s, vb_s, sa_s, sb_s, acc_s,
              *, seq, tq, tk):
    hp = pl.program_id(1)
    qi = pl.program_id(2)
    low = _iota((1, LANES), 1) < HEAD64

    def key_bias(h, r0):
        c_blk = jnp.sum(jnp.where(_iota((1, 16), 1) == 2 * hp + h, cc_ref[0, pl.ds(r0, tk), :], 0.0),
                        axis=1, keepdims=True)
        return c_blk[0:1, :] - c_blk

    @pl.when(qi == 0)
    def _():
        _stage_keys(k_ref, v_ref, kg_ref, kn_s, vb_s, low, seq, tk, key_bias)

    qn = _half_rmsnorm(q_ref[0], qg_ref[...], low) * (HEAD64 ** -0.5)
    sigma_fns = []
    for h in range(2):
        row = 2 * hp + h
        c_q0 = cq_ref[0, row, pl.ds(qi, 1), :][:, 0:1]
        sigma_fns.append(lambda j, row=row, c_q0=c_q0: c_q0 - ck_ref[0, row, pl.ds(j, 1), :][:, 0:1])
    _attend(_augment_queries(qn), kn_s, vb_s, sa_s, sb_s, acc_s, sigma_fns, qi, tq, tk)
    head0_rows = _iota((LANES, 1), 0) < HEAD64
    o_ref[0] = jnp.where(head0_rows, _normalised_output(acc_s[0]),
                         _normalised_output(acc_s[1])).T.astype(o_ref.dtype)


def _fox_cum_body(f_ref, fb_ref, c_ref, *, seq):
    x = f_ref[0] + fb_ref[...]
    logf = jnp.minimum(x, 0.0) - jnp.log1p(jnp.exp(-jnp.abs(x)))
    upper = (_iota((LANES, LANES), 0) <= _iota((LANES, LANES), 1)).astype(F32)
    carry = jnp.zeros((16, 1), F32)
    for blk in range(seq // LANES):
        sl = slice(blk * LANES, (blk + 1) * LANES)
        cb = jnp.dot(logf[:, sl], upper, precision=HI, preferred_element_type=F32) + carry
        c_ref[0, :, sl] = cb
        carry = cb[:, LANES - 1:LANES]


def _attention_scratch(seq, tq, tk):
    return [pltpu.VMEM((2, seq, LANES), BF16), pltpu.VMEM((seq // tk, VT_ROWS, tk), BF16),
            pltpu.VMEM((2, tk, tq), F32), pltpu.VMEM((2, tk, tq), F32), pltpu.VMEM((2, VT_ROWS, tq), F32)]


def _fox(p_fox, f_b, q_g, k_g, col0=0, tq=512, tk=512):
    bsz, seq, _ = p_fox.shape
    tq, tk = min(tq, seq), min(tk, seq)
    cb = col0 // LANES
    f_t = jnp.transpose(p_fox[:, :, col0 + 3 * GROUP:col0 + 3 * GROUP + 16], (0, 2, 1))
    c = pl.pallas_call(
        functools.partial(_fox_cum_body, seq=seq), grid=(bsz,),
        in_specs=[pl.BlockSpec((1, 16, seq), lambda b: (b, 0, 0)), pl.BlockSpec((16, 1), lambda b: (0, 0))],
        out_specs=pl.BlockSpec((1, 16, seq), lambda b: (b, 0, 0)),
        out_shape=jax.ShapeDtypeStruct((bsz, 16, seq), F32), compiler_params=_params(1),
        name="fox_cumsum")(f_t, f_b.reshape(16, 1))
    nb = GROUP // LANES
    gain = lambda g: jnp.tile(g, 2).reshape(1, LANES)
    return pl.pallas_call(
        functools.partial(_fox_body, seq=seq, tq=tq, tk=tk), grid=(bsz, nb, seq // tq),
        in_specs=[pl.BlockSpec((1, tq, LANES), lambda b, h, i: (b, i, cb + h)),
                  pl.BlockSpec((1, seq, LANES), lambda b, h, i: (b, 0, cb + nb + h)),
                  pl.BlockSpec((1, seq, LANES), lambda b, h, i: (b, 0, cb + 2 * nb + h)),
                  pl.BlockSpec((1, 16, seq // tk, tk), lambda b, h, i: (b, 0, 0, 0)),
                  pl.BlockSpec((1, 16, seq // tq, tq), lambda b, h, i: (b, 0, 0, 0)),
                  pl.BlockSpec((1, seq, 16), lambda b, h, i: (b, 0, 0)),
                  pl.BlockSpec((1, LANES), lambda b, h, i: (0, 0)),
                  pl.BlockSpec((1, LANES), lambda b, h, i: (0, 0))],
        out_specs=pl.BlockSpec((1, tq, LANES), lambda b, h, i: (b, i, h)),
        out_shape=jax.ShapeDtypeStruct((bsz, seq, GROUP), BF16),
        scratch_shapes=_attention_scratch(seq, tq, tk),
        compiler_params=_params(3), name="fox_attention")(
            p_fox, p_fox, p_fox, c.reshape(bsz, 16, seq // tk, tk), c.reshape(bsz, 16, seq // tq, tq),
            jnp.transpose(c, (0, 2, 1)), gain(q_g), gain(k_g))


def _diff_body(q_ref, k_ref, v_ref, qg_ref, kg_ref, lam_ref, sg_ref, o_ref, kn_s, vb_s, sa_s, sb_s, acc_s,
               *, seq, tq, tk, lam_init):
    head = pl.program_id(1)
    qi = pl.program_id(2)
    low = _iota((1, LANES), 1) < HEAD64
    slope = jnp.exp2(-(head + 1).astype(F32) * jnp.ones((1, 1), F32))

    @pl.when(qi == 0)
    def _():
        in_block = _iota((tk, 1), 0).astype(F32)
        _stage_keys(k_ref, v_ref, kg_ref, kn_s, vb_s, low, seq, tk, lambda h, r0: slope * in_block)

    lq1, lk1, lq2, lk2 = (lam_ref[i:i + 1, :] for i in range(4))
    lam = (jnp.exp(jnp.sum(lq1 * lk1, axis=1, keepdims=True))
           - jnp.exp(jnp.sum(lq2 * lk2, axis=1, keepdims=True)) + lam_init)

    def sigma(j):
        return slope * (j * tk - qi * tq).astype(F32)

    qn = _half_rmsnorm(q_ref[0], qg_ref[...], low) * (HEAD64 ** -0.5)
    _attend(_augment_queries(qn), kn_s, vb_s, sa_s, sb_s, acc_s, [sigma, sigma], qi, tq, tk)
    o = (_normalised_output(acc_s[0]) - lam * _normalised_output(acc_s[1])).T
    ms = jnp.mean(o * o, axis=1, keepdims=True)
    o_ref[0] = (o * lax.rsqrt(ms + 1e-5) * sg_ref[...] * (1.0 - lam_init)).astype(o_ref.dtype)


def _diff(p_diff, layer, q_g, k_g, lq1, lk1, lq2, lk2, subln_g, col0=0, tq=512, tk=512):
    bsz, seq, _ = p_diff.shape
    tq, tk = min(tq, seq), min(tk, seq)
    nb = GROUP // LANES
    cb = col0 // LANES
    lam_init = 0.8 - 0.6 * math.exp(-0.3 * layer)
    gain = lambda g: jnp.tile(g, 2).reshape(1, LANES)
    lam_rows = jnp.stack([lq1, lk1, lq2, lk2])
    return pl.pallas_call(
        functools.partial(_diff_body, seq=seq, tq=tq, tk=tk, lam_init=lam_init), grid=(bsz, nb, seq // tq),
        in_specs=[pl.BlockSpec((1, tq, LANES), lambda b, h, i: (b, i, cb + h)),
                  pl.BlockSpec((1, seq, LANES), lambda b, h, i: (b, 0, cb + nb + h)),
                  pl.BlockSpec((1, seq, LANES), lambda b, h, i: (b, 0, cb + 2 * nb + h)),
                  pl.BlockSpec((1, LANES), lambda b, h, i: (0, 0)),
                  pl.BlockSpec((1, LANES), lambda b, h, i: (0, 0)),
                  pl.BlockSpec((4, HEAD64), lambda b, h, i: (0, 0)),
                  pl.BlockSpec((1, LANES), lambda b, h, i: (0, 0))],
        out_specs=pl.BlockSpec((1, tq, LANES), lambda b, h, i: (b, i, h)),
        out_shape=jax.ShapeDtypeStruct((bsz, seq, GROUP), BF16),
        scratch_shapes=_attention_scratch(seq, tq, tk),
        compiler_params=_params(3), name="diff_attention")(
            p_diff, p_diff, p_diff, gain(q_g), gain(k_g), lam_rows, subln_g.reshape(1, LANES))


def _gdn_body(q_ref, k_ref, v_ref, qh_ref, kh_ref, vh_ref, gate_ref, cwq_ref, cwk_ref, cwv_ref, alog_ref, dt_ref,
              ng_ref, o_ref, qw_s, au_s, p_s, q_s, eg_s, state_s, *, ts, nheads, unroll):
    pair = 2 * CHUNK
    first_tile = pl.program_id(2) == 0
    lane = _iota((1, LANES), 1)
    pick = lambda x, idx: jnp.sum(jnp.where(lane == idx, x, 0.0), axis=1, keepdims=True)
    m_tril, m_strict, m_upper, m_same = _block_masks()
    eye = _iota((pair, pair), 0) == _iota((pair, pair), 1)
    head_ids = [pl.program_id(1) * nheads + hd for hd in range(nheads)]
    a_scale = [-jnp.exp(pick(alog_ref[...], h)) for h in head_ids]
    dt_bias = [pick(dt_ref[...], h) for h in head_ids]

    @pl.when(first_tile)
    def _():
        state_s[...] = jnp.zeros_like(state_s)

    def prepare(it, carry):
        chains = [(hd, it * unroll + u) for u in range(unroll) for hd in range(nheads)]
        lanes = [slice(hd * LANES, (hd + 1) * LANES) for hd, _ in chains]
        r0s = [pl.multiple_of(cp * pair, pair) for _, cp in chains]

        def conv_silu(ref, halo_ref, cw_ref):
            def one(ln, r0, chain):
                cur = ref[0, pl.ds(r0, pair), ln]
                inside = ref[0, pl.ds(pl.multiple_of(jnp.maximum(r0 - 8, 0), 8), 8), ln]
                before = jnp.where(first_tile, 0.0, halo_ref[0, :, ln])
                ext = jnp.concatenate([jnp.where(chain[1] == 0, before, inside), cur], axis=0)
                acc = cur * cw_ref[GDN_CONV - 1:GDN_CONV, ln]
                for j in range(1, GDN_CONV):
                    acc = acc + pltpu.roll(ext, j, 0)[8:, :] * cw_ref[GDN_CONV - 1 - j:GDN_CONV - j, ln]
                return acc * jax.nn.sigmoid(acc)
            return _each(one, lanes, r0s, chains)

        q = conv_silu(q_ref, qh_ref, cwq_ref)
        k = conv_silu(k_ref, kh_ref, cwk_ref)
        v = conv_silu(v_ref, vh_ref, cwv_ref)
        q = _each(lambda x: x * lax.rsqrt(jnp.sum(x * x, axis=1, keepdims=True) + 1e-6) * (GDN_HEAD ** -0.5), q)
        k = _each(lambda x: x * lax.rsqrt(jnp.sum(x * x, axis=1, keepdims=True) + 1e-6), k)
        gates = [gate_ref[0, pl.ds(r0, pair), :] for r0 in r0s]
        g_col = [a_scale[hd] * _softplus(pick(g, head_ids[hd]) + dt_bias[hd]) for (hd, _), g in zip(chains, gates)]
        beta = [jax.nn.sigmoid(pick(g, head_ids[hd] + 8)) for (hd, _), g in zip(chains, gates)]
        g_row = _each(lambda g: jnp.sum(jnp.where(eye, g, 0.0), axis=0, keepdims=True), g_col)
        gc_col = _each(lambda g: jnp.sum(jnp.where(m_tril, g, 0.0), axis=1, keepdims=True), g_row)
        gc_row = _each(lambda g: jnp.sum(jnp.where(m_upper, g, 0.0), axis=0, keepdims=True), g_col)
        g_last = _each(lambda g: jnp.sum(jnp.where(m_same, g, 0.0), axis=1, keepdims=True), g_row)
        decay = _each(lambda c, r: jnp.exp(jnp.where(m_tril, c - r, -jnp.inf)), gc_col, gc_row)
        exp_gc = _each(jnp.exp, gc_col)
        kb = _each(lambda x, b: x * b, k, beta)
        kk = _each(_mm_nt, kb, k)
        t = _neumann_inverse(_each(lambda x, d: -jnp.where(m_strict, x * d, 0.0), kk, decay))
        uw = _each(lambda t_, v_, b, kb_, e: _mm(t_, jnp.concatenate([v_ * b, kb_ * e], axis=1)),
                   t, v, beta, kb, exp_gc)
        ai = _each(lambda q_, k_, d: _mm_nt(q_, k_) * d, q, k, decay)
        aiuw = _each(_mm, ai, uw)
        kd = _each(lambda k_, gl, gc: k_ * jnp.exp(gl - gc), k, g_last, gc_col)
        kuw = [[_mm_tn(kd_[half * CHUNK:(half + 1) * CHUNK], uw_[half * CHUNK:(half + 1) * CHUNK])
                for kd_, uw_ in zip(kd, uw)] for half in range(2)]
        for i, (hd, cp) in enumerate(chains):
            rows = pl.ds(r0s[i], pair)
            qw_s[hd, rows, :] = (q[i] * exp_gc[i] - aiuw[i][:, LANES:]).astype(BF16)
            au_s[hd, rows, :] = aiuw[i][:, :LANES]
            e_last = jnp.exp(g_last[i])
            for half in range(2):
                c = 2 * cp + half
                sq = pl.ds(pl.multiple_of(c * LANES, LANES), LANES)
                p_s[hd, sq, :] = (-kuw[half][i][:, LANES:]).astype(BF16)
                q_s[hd, sq, :] = kuw[half][i][:, :LANES]
                eg_s[hd, pl.ds(c, 1), :] = jnp.broadcast_to(e_last[half * CHUNK:half * CHUNK + 1], (1, LANES))
        return carry

    lax.fori_loop(0, ts // pair // unroll, prepare, 0)

    def recur(c, states):
        rows = pl.ds(pl.multiple_of(c * CHUNK, CHUNK), CHUNK)
        sq = pl.ds(pl.multiple_of(c * LANES, LANES), LANES)
        sb = [s.astype(BF16) for s in states]
        o = [jnp.dot(qw_s[hd, rows, :], sb[hd], preferred_element_type=F32) + au_s[hd, rows, :]
             for hd in range(nheads)]
        states = tuple(states[hd] * eg_s[hd, pl.ds(c, 1), :]
                       + jnp.dot(p_s[hd, sq, :], sb[hd], preferred_element_type=F32) + q_s[hd, sq, :]
                       for hd in range(nheads))
        for hd in range(nheads):
            ms = jnp.mean(o[hd] * o[hd], axis=1, keepdims=True)
            o_ref[0, rows, hd * LANES:(hd + 1) * LANES] = (
                o[hd] * lax.rsqrt(ms + NORM_EPS) * ng_ref[...]).astype(o_ref.dtype)
        return states

    final = lax.fori_loop(0, ts // CHUNK, recur, tuple(state_s[hd] for hd in range(nheads)))
    for hd in range(nheads):
        state_s[hd] = final[hd]


def _gdn(p_gdn, conv_w, a_log, dt_bias, norm_g, ts=1024, nheads=2, unroll=2):
    bsz, seq, _ = p_gdn.shape
    ts = min(ts, seq)
    nb = GROUP // LANES // nheads
    width = nheads * LANES
    n_chunks = ts // CHUNK
    pad = lambda v: jnp.pad(v, (0, LANES - v.shape[0])).reshape(1, LANES)
    seq_spec = lambda off: pl.BlockSpec((1, ts, width), lambda b, h, s: (b, s, off + h))
    halo_spec = lambda off: pl.BlockSpec((1, 8, width), lambda b, h, s: (b, jnp.maximum(s * (ts // 8) - 1, 0), off + h))
    cw_spec = lambda off: pl.BlockSpec((GDN_CONV, width), lambda b, h, s: (0, off + h))
    row_spec = pl.BlockSpec((1, LANES), lambda b, h, s: (0, 0))
    return pl.pallas_call(
        functools.partial(_gdn_body, ts=ts, nheads=nheads, unroll=unroll), grid=(bsz, nb, seq // ts),
        in_specs=[seq_spec(0), seq_spec(nb), seq_spec(2 * nb), halo_spec(0), halo_spec(nb), halo_spec(2 * nb),
                  pl.BlockSpec((1, ts, LANES), lambda b, h, s: (b, s, 3 * GROUP // LANES)),
                  cw_spec(0), cw_spec(nb), cw_spec(2 * nb), row_spec, row_spec, row_spec],
        out_specs=pl.BlockSpec((1, ts, width), lambda b, h, s: (b, s, h)),
        out_shape=jax.ShapeDtypeStruct((bsz, seq, GROUP), BF16),
        scratch_shapes=[pltpu.VMEM((nheads, ts, LANES), BF16), pltpu.VMEM((nheads, ts, LANES), F32),
                        pltpu.VMEM((nheads, n_chunks * LANES, LANES), BF16),
                        pltpu.VMEM((nheads, n_chunks * LANES, LANES), F32),
                        pltpu.VMEM((nheads, max(n_chunks, 8), LANES), F32),
                        pltpu.VMEM((nheads, GDN_HEAD, GDN_HEAD), F32)],
        compiler_params=_params(3), name="gated_deltanet")(
            p_gdn, p_gdn, p_gdn, p_gdn, p_gdn, p_gdn, p_gdn, conv_w, conv_w, conv_w, pad(a_log), pad(dt_bias),
            norm_g.reshape(1, LANES))


def _rwkv_prep_body(p_ref, halo_ref, mu_ref, w0_ref, wup_ref, a0_ref, aup_ref, kk_ref, ka_ref,
                    r_ref, lw_ref, k_ref, v_ref, kkn_ref, kka_ref):
    p = p_ref[0]
    first = pl.program_id(1) == 0
    last_prev = jnp.where(first, 0.0, halo_ref[0, 7:8, :])
    prev = jnp.where(_iota((p.shape[0], 1), 0) == 0, last_prev, pltpu.roll(p, 1, 0))
    xs = p + mu_ref[...] * (prev - p)
    r, k, v = xs[:, :GROUP], xs[:, GROUP:2 * GROUP], xs[:, 2 * GROUP:3 * GROUP]
    lora = xs[:, 3 * GROUP:]
    w = w0_ref[...] + jnp.dot(jnp.tanh(lora).astype(BF16), wup_ref[...], preferred_element_type=F32)
    a = jax.nn.sigmoid(a0_ref[...] + jnp.dot(lora.astype(BF16), aup_ref[...], preferred_element_type=F32))
    low = _iota((1, LANES), 1) < HEAD64
    r_ref[0] = r
    v_ref[0] = v
    lw_ref[0] = -jnp.exp(-_softplus(-w) - 0.5)
    k_ref[0] = k * (1.0 + (a - 1.0) * ka_ref[...])
    for blk in range(GROUP // LANES):
        sl = slice(blk * LANES, (blk + 1) * LANES)
        kk0 = k[:, sl] * kk_ref[:, sl]
        kkn = kk0 * lax.rsqrt(_half_sum(kk0 * kk0, low) + 1e-6)
        kkn_ref[0, :, sl] = kkn
        kka_ref[0, :, sl] = kkn * a[:, sl]


def _rwkv_body(r_ref, lw_ref, k_ref, v_ref, kk_ref, kka_ref, rk_ref, lng_ref, lnb_ref, o_ref,
               ra_s, y1_s, p_s, q_s, gam_s, state_s, *, ts, npairs, unroll):
    low = _iota((1, LANES), 1) < HEAD64
    m_tril, m_strict, _, _ = _block_masks()
    tril_b = (_iota((CHUNK, CHUNK), 0) >= _iota((CHUNK, CHUNK), 1)).astype(BF16)
    split = lambda x: jnp.concatenate([jnp.where(low, x, 0.0), jnp.where(low, 0.0, x)], axis=0)
    unsplit = lambda x: x[:CHUNK] + x[CHUNK:]
    mul = lambda a, b: a * b

    @pl.when(pl.program_id(2) == 0)
    def _():
        state_s[...] = jnp.zeros_like(state_s)

    def prepare(it, carry):
        chains = [(pr, it * unroll + u) for u in range(unroll) for pr in range(npairs)]
        rows = [pl.ds(pl.multiple_of(c * CHUNK, CHUNK), CHUNK) for _, c in chains]
        lanes = [slice(pr * LANES, (pr + 1) * LANES) for pr, _ in chains]
        ld = lambda ref: [ref[0, rw, ln] for rw, ln in zip(rows, lanes)]
        lw = ld(lw_ref)
        lw_hi = _each(lambda x: x.astype(BF16), lw)
        lw_lo = _each(lambda x, hi: (x - hi.astype(F32)).astype(BF16), lw, lw_hi)
        cum = _each(lambda hi, lo: (jnp.dot(tril_b, hi, preferred_element_type=F32)
                                    + jnp.dot(tril_b, lo, preferred_element_type=F32)), lw_hi, lw_lo)
        gam = _each(jnp.exp, cum)
        inv = _each(lambda c: jnp.exp(-c), cum)
        g_end = _each(lambda g: g[CHUNK - 1:CHUNK, :], gam)
        at = _each(lambda kk, c, w: -kk * jnp.exp(c - w), ld(kk_ref), cum, lw)
        rt = _each(mul, ld(r_ref), gam)
        bt = _each(mul, ld(kka_ref), inv)
        kt = _each(mul, ld(k_ref), inv)
        a_st = _each(split, at)
        v_st = _each(split, ld(v_ref))
        gram = _each(lambda a, r, b, k: _mm_nt(jnp.concatenate([a, split(r)], axis=0),
                                               jnp.concatenate([b, b, k, k], axis=0)), a_st, rt, bt, kt)
        n_ab = _each(lambda g: jnp.where(m_strict, g[:LANES, :LANES], 0.0), gram)
        n_ak = _each(lambda g: jnp.where(m_strict, g[:LANES, LANES:], 0.0), gram)
        n_rb = _each(lambda g: jnp.where(m_tril, g[LANES:, :LANES], 0.0), gram)
        n_rk = _each(lambda g: jnp.where(m_tril, g[LANES:, LANES:], 0.0), gram)
        t = _neumann_inverse(n_ab)
        akv = _each(_mm, n_ak, v_st)
        z = _each(lambda t_, a, x: _mm(t_, jnp.concatenate([a, x], axis=1)), t, a_st, akv)
        rz = _each(_mm, n_rb, z)
        rkv = _each(_mm, n_rk, v_st)
        b_st = _each(lambda b, g: split(b * g), bt, g_end)
        k_st = _each(lambda k, g: split(k * g), kt, g_end)
        pm = _each(lambda b, z_: _mm_tn(b, z_[:, :LANES]), b_st, z)
        qm = _each(lambda b, k, z_, v: _mm_tn(jnp.concatenate([b, k], axis=0),
                                              jnp.concatenate([z_[:, LANES:], v], axis=0)), b_st, k_st, z, v_st)
        for i, (pr, c) in enumerate(chains):
            sq = pl.ds(pl.multiple_of(c * LANES, LANES), LANES)
            ra_s[pr, rows[i], :] = (rt[i] + unsplit(rz[i][:, :LANES])).astype(BF16)
            y1_s[pr, rows[i], :] = unsplit(rkv[i] + rz[i][:, LANES:])
            p_s[pr, sq, :] = pm[i].astype(BF16)
            q_s[pr, sq, :] = qm[i]
            gam_s[pr, sq, :] = jnp.broadcast_to(g_end[i], (LANES, LANES)).T
        return carry

    lax.fori_loop(0, ts // CHUNK // unroll, prepare, 0)

    def recur(c, hs):
        rows = pl.ds(pl.multiple_of(c * CHUNK, CHUNK), CHUNK)
        sq = pl.ds(pl.multiple_of(c * LANES, LANES), LANES)
        hb = [h.astype(BF16) for h in hs]
        y = [jnp.dot(ra_s[pr, rows, :], hb[pr], preferred_element_type=F32) + y1_s[pr, rows, :]
             for pr in range(npairs)]
        hs = tuple(gam_s[pr, sq, :] * hs[pr] + jnp.dot(p_s[pr, sq, :], hb[pr], preferred_element_type=F32)
                   + q_s[pr, sq, :] for pr in range(npairs))
        for pr in range(npairs):
            ln = slice(pr * LANES, (pr + 1) * LANES)
            mean = _half_sum(y[pr], low) * (1.0 / HEAD64)
            yc = y[pr] - mean
            var = _half_sum(yc * yc, low) * (1.0 / HEAD64)
            rkk = r_ref[0, rows, ln] * k_ref[0, rows, ln] * rk_ref[:, ln]
            out = (yc * lax.rsqrt(var + RWKV_GN_EPS) * lng_ref[:, ln] + lnb_ref[:, ln]
                   + _half_sum(rkk, low) * v_ref[0, rows, ln])
            o_ref[0, rows, ln] = out.astype(o_ref.dtype)
        return hs

    final = lax.fori_loop(0, ts // CHUNK, recur, tuple(state_s[pr] for pr in range(npairs)))
    for pr in range(npairs):
        state_s[pr] = final[pr]


def _rwkv(p_rwkv, mu, w0, w_up, a0, a_up, k_k, k_a, r_k, ln_g, ln_b, tm=256, ts=1024, npairs=2, unroll=4):
    bsz, seq, _ = p_rwkv.shape
    tm, ts = min(tm, seq), min(ts, seq)
    unroll = min(unroll, ts // CHUNK)
    width = npairs * LANES
    nb = GROUP // width
    zeros = jnp.zeros((LORA, GROUP), F32)
    wup = jnp.concatenate([w_up, zeros], axis=0).astype(BF16)
    aup = jnp.concatenate([zeros, a_up], axis=0).astype(BF16)
    row = lambda v: v.reshape(1, -1)
    full = lambda n: pl.BlockSpec((1, n), lambda b, i: (0, 0))
    out_spec = pl.BlockSpec((1, tm, GROUP), lambda b, i: (b, i, 0))
    seq_f32 = jax.ShapeDtypeStruct((bsz, seq, GROUP), F32)
    r, lw, k, v, kk, kka = pl.pallas_call(
        _rwkv_prep_body, grid=(bsz, seq // tm),
        in_specs=[pl.BlockSpec((1, tm, N_RWKV), lambda b, i: (b, i, 0)),
                  pl.BlockSpec((1, 8, N_RWKV), lambda b, i: (b, jnp.maximum(i * (tm // 8) - 1, 0), 0)),
                  full(N_RWKV), full(GROUP),
                  pl.BlockSpec((2 * LORA, GROUP), lambda b, i: (0, 0)), full(GROUP),
                  pl.BlockSpec((2 * LORA, GROUP), lambda b, i: (0, 0)), full(GROUP), full(GROUP)],
        out_specs=[out_spec] * 6, out_shape=[seq_f32] * 6, compiler_params=_params(2),
        name="rwkv_prep")(p_rwkv, p_rwkv, row(mu), row(w0), wup, row(a0), aup, row(k_k), row(k_a))
    n_chunks = ts // CHUNK
    seq_spec = pl.BlockSpec((1, ts, width), lambda b, h, s: (b, s, h))
    par_spec = pl.BlockSpec((1, width), lambda b, h, s: (0, h))
    return pl.pallas_call(
        functools.partial(_rwkv_body, ts=ts, npairs=npairs, unroll=unroll), grid=(bsz, nb, seq // ts),
        in_specs=[seq_spec] * 6 + [par_spec] * 3,
        out_specs=seq_spec, out_shape=jax.ShapeDtypeStruct((bsz, seq, GROUP), BF16),
        scratch_shapes=[pltpu.VMEM((npairs, ts, LANES), BF16), pltpu.VMEM((npairs, ts, LANES), F32),
                        pltpu.VMEM((npairs, n_chunks * LANES, LANES), BF16),
                        pltpu.VMEM((npairs, n_chunks * LANES, LANES), F32),
                        pltpu.VMEM((npairs, n_chunks * LANES, LANES), F32),
                        pltpu.VMEM((npairs, LANES, LANES), F32)],
        compiler_params=_params(3), name="rwkv7")(
            r, lw, k, v, kk, kka, row(r_k), row(ln_g), row(ln_b))


def _split_w_in(w_in):
    b0, b1, b2, b3 = N_RWKV, N_RWKV + N_FOX, N_RWKV + N_FOX + N_GDN, N_RWKV + N_FOX + N_GDN + N_DIFF
    pad = lambda w, n: jnp.pad(w, ((0, 0), (0, 0), (0, n - w.shape[-1])))
    wb = w_in.astype(BF16)
    w_a = jnp.concatenate([wb[..., :b0], pad(wb[..., b0:b1], SEG_PAD)], axis=-1)
    w_b = jnp.concatenate([pad(wb[..., b1:b2], GDN_PAD), wb[..., b2:b3]], axis=-1)
    return w_a, w_b, wb[..., b3:]


def kernel(x, norm_g, w_in, w_out, rwkv_mu, rwkv_w0, rwkv_w_up, rwkv_a0, rwkv_a_up, rwkv_k_k, rwkv_k_a, rwkv_r_k, rwkv_ln_g, rwkv_ln_b, fox_q_g, fox_k_g, fox_f_b, gdn_conv, gdn_a_log, gdn_dt_bias, gdn_norm_g, diff_q_g, diff_k_g, diff_lq1, diff_lk1, diff_lq2, diff_lk2, diff_subln_g):
    bsz, seq, d = x.shape
    m = bsz * seq
    w_a, w_b, w_z = _split_w_in(w_in)
    w_out_b = w_out.astype(BF16)
    x2d = x.reshape(m, d)
    for l in range(w_in.shape[0]):
        h = _rmsnorm(x2d, norm_g[l])
        p_a = _matmul(h, w_a[l], IN_PROJ_TN, F32).reshape(bsz, seq, -1)
        p_b = _matmul(h, w_b[l], IN_PROJ_TN, F32).reshape(bsz, seq, -1)
        sz = _matmul(h, w_z[l], 1024, BF16, silu=True, name="in_proj_gate")
        y_rwkv = _rwkv(p_a, rwkv_mu[l], rwkv_w0[l], rwkv_w_up[l], rwkv_a0[l], rwkv_a_up[l], rwkv_k_k[l],
                       rwkv_k_a[l], rwkv_r_k[l].reshape(-1), rwkv_ln_g[l], rwkv_ln_b[l])
        y_fox = _fox(p_a, fox_f_b[l], fox_q_g[l], fox_k_g[l], col0=N_RWKV)
        y_gdn = _gdn(p_b, gdn_conv[l], gdn_a_log[l], gdn_dt_bias[l], gdn_norm_g[l])
        y_diff = _diff(p_b, l, diff_q_g[l], diff_k_g[l], diff_lq1[l], diff_lk1[l], diff_lq2[l], diff_lk2[l],
                       diff_subln_g[l], col0=GDN_PAD)
        ys = [y.reshape(m, GROUP) for y in (y_rwkv, y_fox, y_gdn, y_diff)]
        x2d = _outproj(x2d, ys, sz, w_out_b[l])
    return x2d.reshape(bsz, seq, d)
```

```python
import functools
import math

import jax
import jax.numpy as jnp
from jax import lax
from jax.experimental import pallas as pl
from jax.experimental.pallas import tpu as pltpu

F32 = jnp.float32
BF16 = jnp.bfloat16
HI = lax.Precision.HIGHEST

D_MODEL = 2048
DEPTH = 4
GROUP = 1024
D_MIX = 4 * GROUP
LANES = 128
HEAD64 = 64
GDN_HEAD = 128
CHUNK = 64
LORA = 64
RWKV_GN_EPS = 64e-5
NORM_EPS = 1e-6
GDN_CONV = 4
N_RWKV = 3 * GROUP + 2 * LORA
N_FOX = 3 * GROUP + 16
N_GDN = 3 * GROUP + 16
N_DIFF = 3 * GROUP
SEG_PAD = 3200
GDN_PAD = 3328
IN_PROJ_TN = 1280
VMEM_LIMIT = 56 * 1024 * 1024

NT = (((1,), (1,)), ((), ()))
TN = (((0,), (0,)), ((), ()))


def _params(n_axes):
    return pltpu.CompilerParams(dimension_semantics=("arbitrary",) * n_axes,
                                vmem_limit_bytes=VMEM_LIMIT)


def _softplus(x):
    return jnp.maximum(x, 0.0) + jnp.log1p(jnp.exp(-jnp.abs(x)))


def _iota(shape, axis):
    return lax.broadcasted_iota(jnp.int32, shape, axis)


def _each(fn, *lists):
    return [fn(*args) for args in zip(*lists)]


def _half_sum(x, low):
    s0 = jnp.sum(jnp.where(low, x, 0.0), axis=1, keepdims=True)
    s1 = jnp.sum(jnp.where(low, 0.0, x), axis=1, keepdims=True)
    return jnp.where(low, s0, s1)


def _mm(a, b):
    return jnp.dot(a.astype(BF16), b.astype(BF16), preferred_element_type=F32)


def _mm_nt(a, b):
    return lax.dot_general(a.astype(BF16), b.astype(BF16), NT, preferred_element_type=F32)


def _mm_tn(a, b):
    return lax.dot_general(a.astype(BF16), b.astype(BF16), TN, preferred_element_type=F32)


def _neumann_inverse(ns):
    eye = (_iota((LANES, LANES), 0) == _iota((LANES, LANES), 1)).astype(F32)
    ts = [eye + n for n in ns]
    ps = list(ns)
    for _ in range(5):
        ps = _each(_mm, ps, ps)
        ts = _each(lambda t, p: t + _mm(t, p), ts, ps)
    return ts


def _block_masks():
    ri = _iota((LANES, LANES), 0)
    ci = _iota((LANES, LANES), 1)
    same = (ri < CHUNK) == (ci < CHUNK)
    return same & (ri >= ci), same & (ri > ci), same & (ri <= ci), same


def _rmsnorm_body(x_ref, g_ref, o_ref):
    x = x_ref[...]
    ms = jnp.mean(x * x, axis=-1, keepdims=True)
    o_ref[...] = (x * lax.rsqrt(ms + NORM_EPS) * g_ref[...]).astype(o_ref.dtype)


def _rmsnorm(x2d, g, tm=512):
    m, d = x2d.shape
    return pl.pallas_call(
        _rmsnorm_body, grid=(m // tm,),
        in_specs=[pl.BlockSpec((tm, d), lambda i: (i, 0)), pl.BlockSpec((1, d), lambda i: (0, 0))],
        out_specs=pl.BlockSpec((tm, d), lambda i: (i, 0)),
        out_shape=jax.ShapeDtypeStruct((m, d), BF16), compiler_params=_params(1),
        name="rmsnorm")(x2d, g.reshape(1, d))


def _matmul_body(h_ref, w_ref, o_ref, *, silu):
    acc = jnp.dot(h_ref[...], w_ref[...], preferred_element_type=F32)
    if silu:
        acc = acc * jax.nn.sigmoid(acc)
    o_ref[...] = acc.astype(o_ref.dtype)


def _matmul(h, w, tn, out_dtype, silu=False, tm=1024, name="in_proj"):
    m, k = h.shape
    n = w.shape[1]
    tm = min(tm, m)
    return pl.pallas_call(
        functools.partial(_matmul_body, silu=silu), grid=(m // tm, n // tn),
        in_specs=[pl.BlockSpec((tm, k), lambda i, j: (i, 0)), pl.BlockSpec((k, tn), lambda i, j: (0, j))],
        out_specs=pl.BlockSpec((tm, tn), lambda i, j: (i, j)),
        out_shape=jax.ShapeDtypeStruct((m, n), out_dtype), compiler_params=_params(2),
        name=name)(h, w)


def _outproj_body(x_ref, yr_ref, yf_ref, yg_ref, yd_ref, sz_ref, w_ref, o_ref):
    acc = x_ref[...]
    for g, y_ref in enumerate((yr_ref, yf_ref, yg_ref, yd_ref)):
        gate = (y_ref[...].astype(F32) * sz_ref[:, g * GROUP:(g + 1) * GROUP].astype(F32)).astype(BF16)
        acc = acc + jnp.dot(gate, w_ref[g * GROUP:(g + 1) * GROUP, :], preferred_element_type=F32)
    o_ref[...] = acc


def _outproj(x2d, ys, sz, w_out, tm=512, tn=1024):
    m, d = x2d.shape
    tm = min(tm, m)
    yspec = pl.BlockSpec((tm, GROUP), lambda j, i: (i, 0))
    return pl.pallas_call(
        _outproj_body, grid=(d // tn, m // tm),
        in_specs=[pl.BlockSpec((tm, tn), lambda j, i: (i, j)), yspec, yspec, yspec, yspec,
                  pl.BlockSpec((tm, D_MIX), lambda j, i: (i, 0)),
                  pl.BlockSpec((D_MIX, tn), lambda j, i: (0, j))],
        out_specs=pl.BlockSpec((tm, tn), lambda j, i: (i, j)),
        out_shape=jax.ShapeDtypeStruct((m, d), F32), compiler_params=_params(2),
        name="out_proj")(x2d, *ys, sz, w_out)


def _aug_lanes(h):
    first = (1 - h) * HEAD64
    lane = _iota((1, LANES), 1)
    return lane == first, lane == first + 1


ONES_ROWS = 16
LOG2E = 1.4426950408889634


def _attend(qts, kn_ref, vt_ref, sa_ref, sb_ref, acc_ref, sigma_fns, qi, tq, tk):
    q_pos = qi * tq + _iota((1, tq), 1)
    heads = range(len(qts))

    def produce(j, s_ref, diagonal=False):
        k0 = pl.multiple_of(j * tk, tk)
        s = [jnp.dot(kn_ref[h, pl.ds(k0, tk), :], qts[h], preferred_element_type=F32) for h in heads]
        if diagonal:
            visible = (k0 + _iota((tk, 1), 0)) <= q_pos
            s = [jnp.where(visible, x, -jnp.inf) for x in s]
        for h in heads:
            s_ref[h] = s[h]
        return [jnp.max(x, axis=0, keepdims=True) for x in s]

    def consume(j, s_ref, s_max, m):
        sigma = [sigma_fns[h](j) * LOG2E for h in heads]
        m_new = [jnp.maximum(m[h], s_max[h] + sigma[h]) for h in heads]
        alpha = [jnp.exp2(m[h] - m_new[h]) for h in heads]
        p = [jnp.exp2(s_ref[h] - (m_new[h] - sigma[h])).astype(BF16) for h in heads]
        pv = [jnp.dot(vt_ref[h % vt_ref.shape[0], j], p[h], preferred_element_type=F32) for h in heads]
        for h in heads:
            acc_ref[h] = alpha[h] * acc_ref[h] + pv[h]
        return m_new

    n_full = (qi * tq) // tk
    acc_ref[...] = jnp.zeros_like(acc_ref)
    max_a = produce(n_full, sa_ref, diagonal=True)

    def pair(t, carry):
        m, max_a, in_a = carry
        max_b = produce(2 * t, sb_ref)
        m = consume(in_a, sa_ref, max_a, m)
        max_a = produce(2 * t + 1, sa_ref)
        m = consume(2 * t, sb_ref, max_b, m)
        return m, max_a, 2 * t + 1

    m, max_a, in_a = lax.fori_loop(0, n_full // 2, pair,
                                   ([jnp.full((1, tq), -jnp.inf, F32)] * 2, max_a, n_full))

    @pl.when(n_full % 2 == 1)
    def _():
        max_b = produce(n_full - 1, sb_ref)
        consume(n_full - 1, sb_ref, max_b, consume(in_a, sa_ref, max_a, m))

    @pl.when(n_full % 2 == 0)
    def _():
        consume(in_a, sa_ref, max_a, m)


def _half_rmsnorm(x, gain, low):
    ms = _half_sum(x * x, low) * (1.0 / HEAD64)
    return x * lax.rsqrt(ms + NORM_EPS) * gain


def _augment_queries(qn):
    qt = qn.T
    row = _iota((LANES, 1), 0)
    out = []
    for h in range(2):
        first = (1 - h) * HEAD64
        own = row < HEAD64 if h == 0 else row >= HEAD64
        out.append(jnp.where(own, qt, jnp.where((row == first) | (row == first + 1), 1.0, 0.0)).astype(BF16))
    return out


def _normalised_output(acc):
    dv = acc.shape[0] - ONES_ROWS
    return acc[:dv] / acc[dv:dv + 1]


def _stage_keys(k_ref, v_ref, kg_ref, kn_s, vt_s, low, seq, tk, key_bias):
    def body(i, c):
        r0 = pl.multiple_of(i * tk, tk)
        rows = pl.ds(r0, tk)
        kn = _half_rmsnorm(k_ref[0, rows, :], kg_ref[...], low)
        for h in range(2):
            d = key_bias(h, r0) * LOG2E
            d_hi = d.astype(BF16).astype(F32)
            a0, a1 = _aug_lanes(h)
            aug = jnp.where(a0, d_hi, jnp.where(a1, d - d_hi, 0.0))
            kn_s[h, rows, :] = jnp.where(low if h == 0 else ~low, kn, aug).astype(BF16)
        vt = v_ref[0, rows, :].T.astype(BF16)
        dv = vt_s.shape[2] - ONES_ROWS
        for g in range(vt_s.shape[0]):
            vt_s[g, i, :dv, :] = vt[g * dv:(g + 1) * dv]
            vt_s[g, i, dv:, :] = jnp.ones((ONES_ROWS, tk), BF16)
        return c

    lax.fori_loop(0, seq // tk, body, 0)


def _fox_body(q_ref, k_ref, v_ref, ck_ref, cq_ref, cc_ref, qg_ref, kg_ref, o_ref, kn_s, vb_s, sa_s, sb_s, acc_s,
              *, seq, tq, tk):
    hp = pl.program_id(1)
    qi = pl.program_id(2)
    low = _iota((1, LANES), 1) < HEAD64

    def key_bias(h, r0):
        c_blk = jnp.sum(jnp.where(_iota((1, 16), 1) == 2 * hp + h, cc_ref[0, pl.ds(r0, tk), :], 0.0),
                        axis=1, keepdims=True)
        return c_blk[0:1, :] - c_blk

    @pl.when(qi == 0)
    def _():
        _stage_keys(k_ref, v_ref, kg_ref, kn_s, vb_s, low, seq, tk, key_bias)

    qn = _half_rmsnorm(q_ref[0], qg_ref[...], low) * (HEAD64 ** -0.5 * LOG2E)
    sigma_fns = []
    for h in range(2):
        row = 2 * hp + h
        c_q0 = cq_ref[0, row, pl.ds(qi, 1), :][:, 0:1]
        sigma_fns.append(lambda j, row=row, c_q0=c_q0: c_q0 - ck_ref[0, row, pl.ds(j, 1), :][:, 0:1])
    _attend(_augment_queries(qn), kn_s, vb_s, sa_s, sb_s, acc_s, sigma_fns, qi, tq, tk)
    o_ref[0] = jnp.concatenate([_normalised_output(acc_s[0]), _normalised_output(acc_s[1])],
                               axis=0).T.astype(o_ref.dtype)


def _fox_cum_body(f_ref, fb_ref, c_ref, *, seq):
    x = f_ref[0] + fb_ref[...]
    logf = jnp.minimum(x, 0.0) - jnp.log1p(jnp.exp(-jnp.abs(x)))
    upper = (_iota((LANES, LANES), 0) <= _iota((LANES, LANES), 1)).astype(F32)
    carry = jnp.zeros((16, 1), F32)
    for blk in range(seq // LANES):
        sl = slice(blk * LANES, (blk + 1) * LANES)
        cb = jnp.dot(logf[:, sl], upper, precision=HI, preferred_element_type=F32) + carry
        c_ref[0, :, sl] = cb
        carry = cb[:, LANES - 1:LANES]


def _attention_scratch(seq, tq, tk, dv):
    return [pltpu.VMEM((2, seq, LANES), BF16), pltpu.VMEM((LANES // dv, seq // tk, dv + ONES_ROWS, tk), BF16),
            pltpu.VMEM((2, tk, tq), F32), pltpu.VMEM((2, tk, tq), F32),
            pltpu.VMEM((2, dv + ONES_ROWS, tq), F32)]


def _fox(p_fox, f_b, q_g, k_g, col0=0, tq=512, tk=512):
    bsz, seq, _ = p_fox.shape
    tq, tk = min(tq, seq), min(tk, seq)
    cb = col0 // LANES
    f_t = jnp.transpose(p_fox[:, :, col0 + 3 * GROUP:col0 + 3 * GROUP + 16], (0, 2, 1))
    c = pl.pallas_call(
        functools.partial(_fox_cum_body, seq=seq), grid=(bsz,),
        in_specs=[pl.BlockSpec((1, 16, seq), lambda b: (b, 0, 0)), pl.BlockSpec((16, 1), lambda b: (0, 0))],
        out_specs=pl.BlockSpec((1, 16, seq), lambda b: (b, 0, 0)),
        out_shape=jax.ShapeDtypeStruct((bsz, 16, seq), F32), compiler_params=_params(1),
        name="fox_cumsum")(f_t, f_b.reshape(16, 1))
    nb = GROUP // LANES
    gain = lambda g: jnp.tile(g, 2).reshape(1, LANES)
    return pl.pallas_call(
        functools.partial(_fox_body, seq=seq, tq=tq, tk=tk), grid=(bsz, nb, seq // tq),
        in_specs=[pl.BlockSpec((1, tq, LANES), lambda b, h, i: (b, i, cb + h)),
                  pl.BlockSpec((1, seq, LANES), lambda b, h, i: (b, 0, cb + nb + h)),
                  pl.BlockSpec((1, seq, LANES), lambda b, h, i: (b, 0, cb + 2 * nb + h)),
                  pl.BlockSpec((1, 16, seq // tk, tk), lambda b, h, i: (b, 0, 0, 0)),
                  pl.BlockSpec((1, 16, seq // tq, tq), lambda b, h, i: (b, 0, 0, 0)),
                  pl.BlockSpec((1, seq, 16), lambda b, h, i: (b, 0, 0)),
                  pl.BlockSpec((1, LANES), lambda b, h, i: (0, 0)),
                  pl.BlockSpec((1, LANES), lambda b, h, i: (0, 0))],
        out_specs=pl.BlockSpec((1, tq, LANES), lambda b, h, i: (b, i, h)),
        out_shape=jax.ShapeDtypeStruct((bsz, seq, GROUP), BF16),
        scratch_shapes=_attention_scratch(seq, tq, tk, HEAD64),
        compiler_params=_params(3), name="fox_attention")(
            p_fox, p_fox, p_fox, c.reshape(bsz, 16, seq // tk, tk), c.reshape(bsz, 16, seq // tq, tq),
            jnp.transpose(c, (0, 2, 1)), gain(q_g), gain(k_g))


def _diff_body(q_ref, k_ref, v_ref, qg_ref, kg_ref, lam_ref, sg_ref, o_ref, kn_s, vb_s, sa_s, sb_s, acc_s,
               *, seq, tq, tk, lam_init):
    head = pl.program_id(1)
    qi = pl.program_id(2)
    low = _iota((1, LANES), 1) < HEAD64
    slope = jnp.exp2(-(head + 1).astype(F32) * jnp.ones((1, 1), F32))

    @pl.when(qi == 0)
    def _():
        in_block = _iota((tk, 1), 0).astype(F32)
        _stage_keys(k_ref, v_ref, kg_ref, kn_s, vb_s, low, seq, tk, lambda h, r0: slope * in_block)

    lq1, lk1, lq2, lk2 = (lam_ref[i:i + 1, :] for i in range(4))
    lam = (jnp.exp(jnp.sum(lq1 * lk1, axis=1, keepdims=True))
           - jnp.exp(jnp.sum(lq2 * lk2, axis=1, keepdims=True)) + lam_init)

    def sigma(j):
        return slope * (j * tk - qi * tq).astype(F32)

    qn = _half_rmsnorm(q_ref[0], qg_ref[...], low) * (HEAD64 ** -0.5 * LOG2E)
    _attend(_augment_queries(qn), kn_s, vb_s, sa_s, sb_s, acc_s, [sigma, sigma], qi, tq, tk)
    o = (_normalised_output(acc_s[0]) - lam * _normalised_output(acc_s[1])).T
    ms = jnp.mean(o * o, axis=1, keepdims=True)
    o_ref[0] = (o * lax.rsqrt(ms + 1e-5) * sg_ref[...] * (1.0 - lam_init)).astype(o_ref.dtype)


def _diff(p_diff, layer, q_g, k_g, lq1, lk1, lq2, lk2, subln_g, col0=0, tq=512, tk=512):
    bsz, seq, _ = p_diff.shape
    tq, tk = min(tq, seq), min(tk, seq)
    nb = GROUP // LANES
    cb = col0 // LANES
    lam_init = 0.8 - 0.6 * math.exp(-0.3 * layer)
    gain = lambda g: jnp.tile(g, 2).reshape(1, LANES)
    lam_rows = jnp.stack([lq1, lk1, lq2, lk2])
    return pl.pallas_call(
        functools.partial(_diff_body, seq=seq, tq=tq, tk=tk, lam_init=lam_init), grid=(bsz, nb, seq // tq),
        in_specs=[pl.BlockSpec((1, tq, LANES), lambda b, h, i: (b, i, cb + h)),
                  pl.BlockSpec((1, seq, LANES), lambda b, h, i: (b, 0, cb + nb + h)),
                  pl.BlockSpec((1, seq, LANES), lambda b, h, i: (b, 0, cb + 2 * nb + h)),
                  pl.BlockSpec((1, LANES), lambda b, h, i: (0, 0)),
                  pl.BlockSpec((1, LANES), lambda b, h, i: (0, 0)),
                  pl.BlockSpec((4, HEAD64), lambda b, h, i: (0, 0)),
                  pl.BlockSpec((1, LANES), lambda b, h, i: (0, 0))],
        out_specs=pl.BlockSpec((1, tq, LANES), lambda b, h, i: (b, i, h)),
        out_shape=jax.ShapeDtypeStruct((bsz, seq, GROUP), BF16),
        scratch_shapes=_attention_scratch(seq, tq, tk, 2 * HEAD64),
        compiler_params=_params(3), name="diff_attention")(
            p_diff, p_diff, p_diff, gain(q_g), gain(k_g), lam_rows, subln_g.reshape(1, LANES))


def _gdn_body(q_ref, k_ref, v_ref, qh_ref, kh_ref, vh_ref, gate_ref, cwq_ref, cwk_ref, cwv_ref, alog_ref, dt_ref,
              ng_ref, o_ref, qw_s, au_s, p_s, q_s, eg_s, state_s, *, ts, nheads, unroll):
    pair = 2 * CHUNK
    first_tile = pl.program_id(2) == 0
    lane = _iota((1, LANES), 1)
    pick = lambda x, idx: jnp.sum(jnp.where(lane == idx, x, 0.0), axis=1, keepdims=True)
    m_tril, m_strict, m_upper, m_same = _block_masks()
    eye = _iota((pair, pair), 0) == _iota((pair, pair), 1)
    head_ids = [pl.program_id(1) * nheads + hd for hd in range(nheads)]
    a_scale = [-jnp.exp(pick(alog_ref[...], h)) for h in head_ids]
    dt_bias = [pick(dt_ref[...], h) for h in head_ids]

    @pl.when(first_tile)
    def _():
        state_s[...] = jnp.zeros_like(state_s)

    def prepare(it, carry):
        chains = [(hd, it * unroll + u) for u in range(unroll) for hd in range(nheads)]
        lanes = [slice(hd * LANES, (hd + 1) * LANES) for hd, _ in chains]
        r0s = [pl.multiple_of(cp * pair, pair) for _, cp in chains]

        def conv_silu(ref, halo_ref, cw_ref):
            def one(ln, r0, chain):
                cur = ref[0, pl.ds(r0, pair), ln]
                inside = ref[0, pl.ds(pl.multiple_of(jnp.maximum(r0 - 8, 0), 8), 8), ln]
                before = jnp.where(first_tile, 0.0, halo_ref[0, :, ln])
                ext = jnp.concatenate([jnp.where(chain[1] == 0, before, inside), cur], axis=0)
                acc = cur * cw_ref[GDN_CONV - 1:GDN_CONV, ln]
                for j in range(1, GDN_CONV):
                    acc = acc + pltpu.roll(ext, j, 0)[8:, :] * cw_ref[GDN_CONV - 1 - j:GDN_CONV - j, ln]
                return acc * jax.nn.sigmoid(acc)
            return _each(one, lanes, r0s, chains)

        q = conv_silu(q_ref, qh_ref, cwq_ref)
        k = conv_silu(k_ref, kh_ref, cwk_ref)
        v = conv_silu(v_ref, vh_ref, cwv_ref)
        q = _each(lambda x: x * lax.rsqrt(jnp.sum(x * x, axis=1, keepdims=True) + 1e-6) * (GDN_HEAD ** -0.5), q)
        k = _each(lambda x: x * lax.rsqrt(jnp.sum(x * x, axis=1, keepdims=True) + 1e-6), k)
        gates = [gate_ref[0, pl.ds(r0, pair), :] for r0 in r0s]
        g_col = [a_scale[hd] * _softplus(pick(g, head_ids[hd]) + dt_bias[hd]) for (hd, _), g in zip(chains, gates)]
        beta = [jax.nn.sigmoid(pick(g, head_ids[hd] + 8)) for (hd, _), g in zip(chains, gates)]
        g_row = _each(lambda g: jnp.sum(jnp.where(eye, g, 0.0), axis=0, keepdims=True), g_col)
        gc_col = _each(lambda g: jnp.sum(jnp.where(m_tril, g, 0.0), axis=1, keepdims=True), g_row)
        gc_row = _each(lambda g: jnp.sum(jnp.where(m_upper, g, 0.0), axis=0, keepdims=True), g_col)
        g_last = _each(lambda g: jnp.sum(jnp.where(m_same, g, 0.0), axis=1, keepdims=True), g_row)
        decay = _each(lambda c, r: jnp.exp(jnp.where(m_tril, c - r, -jnp.inf)), gc_col, gc_row)
        exp_gc = _each(jnp.exp, gc_col)
        kb = _each(lambda x, b: x * b, k, beta)
        kk = _each(_mm_nt, kb, k)
        t = _neumann_inverse(_each(lambda x, d: -jnp.where(m_strict, x * d, 0.0), kk, decay))
        uw = _each(lambda t_, v_, b, kb_, e: _mm(t_, jnp.concatenate([v_ * b, kb_ * e], axis=1)),
                   t, v, beta, kb, exp_gc)
        ai = _each(lambda q_, k_, d: _mm_nt(q_, k_) * d, q, k, decay)
        aiuw = _each(_mm, ai, uw)
        kd = _each(lambda k_, gl, gc: k_ * jnp.exp(gl - gc), k, g_last, gc_col)
        kuw = [[_mm_tn(kd_[half * CHUNK:(half + 1) * CHUNK], uw_[half * CHUNK:(half + 1) * CHUNK])
                for kd_, uw_ in zip(kd, uw)] for half in range(2)]
        for i, (hd, cp) in enumerate(chains):
            rows = pl.ds(r0s[i], pair)
            qw_s[hd, rows, :] = (q[i] * exp_gc[i] - aiuw[i][:, LANES:]).astype(BF16)
            au_s[hd, rows, :] = aiuw[i][:, :LANES]
            e_last = jnp.exp(g_last[i])
            for half in range(2):
                c = 2 * cp + half
                sq = pl.ds(pl.multiple_of(c * LANES, LANES), LANES)
                p_s[hd, sq, :] = (-kuw[half][i][:, LANES:]).astype(BF16)
                q_s[hd, sq, :] = kuw[half][i][:, :LANES]
                eg_s[hd, pl.ds(c, 1), :] = jnp.broadcast_to(e_last[half * CHUNK:half * CHUNK + 1], (1, LANES))
        return carry

    lax.fori_loop(0, ts // pair // unroll, prepare, 0)

    def recur(c, states):
        rows = pl.ds(pl.multiple_of(c * CHUNK, CHUNK), CHUNK)
        sq = pl.ds(pl.multiple_of(c * LANES, LANES), LANES)
        sb = [s.astype(BF16) for s in states]
        for hd in range(nheads):
            au_s[hd, rows, :] = jnp.dot(qw_s[hd, rows, :], sb[hd], preferred_element_type=F32) + au_s[hd, rows, :]
        return tuple(states[hd] * eg_s[hd, pl.ds(c, 1), :]
                     + jnp.dot(p_s[hd, sq, :], sb[hd], preferred_element_type=F32) + q_s[hd, sq, :]
                     for hd in range(nheads))

    final = lax.fori_loop(0, ts // CHUNK, recur, tuple(state_s[hd] for hd in range(nheads)))
    for hd in range(nheads):
        state_s[hd] = final[hd]

    def finish(i, carry):
        rows = pl.ds(pl.multiple_of(i * 4 * CHUNK, 4 * CHUNK), 4 * CHUNK)
        for hd in range(nheads):
            o = au_s[hd, rows, :]
            ms = jnp.mean(o * o, axis=1, keepdims=True)
            o_ref[0, rows, hd * LANES:(hd + 1) * LANES] = (o * lax.rsqrt(ms + NORM_EPS) * ng_ref[...]).astype(o_ref.dtype)
        return carry

    lax.fori_loop(0, ts // (4 * CHUNK), finish, 0)


def _gdn(p_gdn, conv_w, a_log, dt_bias, norm_g, ts=1024, nheads=2, unroll=2):
    bsz, seq, _ = p_gdn.shape
    ts = min(ts, seq)
    nb = GROUP // LANES // nheads
    width = nheads * LANES
    n_chunks = ts // CHUNK
    pad = lambda v: jnp.pad(v, (0, LANES - v.shape[0])).reshape(1, LANES)
    seq_spec = lambda off: pl.BlockSpec((1, ts, width), lambda b, h, s: (b, s, off + h))
    halo_spec = lambda off: pl.BlockSpec((1, 8, width), lambda b, h, s: (b, jnp.maximum(s * (ts // 8) - 1, 0), off + h))
    cw_spec = lambda off: pl.BlockSpec((GDN_CONV, width), lambda b, h, s: (0, off + h))
    row_spec = pl.BlockSpec((1, LANES), lambda b, h, s: (0, 0))
    return pl.pallas_call(
        functools.partial(_gdn_body, ts=ts, nheads=nheads, unroll=unroll), grid=(bsz, nb, seq // ts),
        in_specs=[seq_spec(0), seq_spec(nb), seq_spec(2 * nb), halo_spec(0), halo_spec(nb), halo_spec(2 * nb),
                  pl.BlockSpec((1, ts, LANES), lambda b, h, s: (b, s, 3 * GROUP // LANES)),
                  cw_spec(0), cw_spec(nb), cw_spec(2 * nb), row_spec, row_spec, row_spec],
        out_specs=pl.BlockSpec((1, ts, width), lambda b, h, s: (b, s, h)),
        out_shape=jax.ShapeDtypeStruct((bsz, seq, GROUP), BF16),
        scratch_shapes=[pltpu.VMEM((nheads, ts, LANES), BF16), pltpu.VMEM((nheads, ts, LANES), F32),
                        pltpu.VMEM((nheads, n_chunks * LANES, LANES), BF16),
                        pltpu.VMEM((nheads, n_chunks * LANES, LANES), F32),
                        pltpu.VMEM((nheads, max(n_chunks, 8), LANES), F32),
                        pltpu.VMEM((nheads, GDN_HEAD, GDN_HEAD), F32)],
        compiler_params=_params(3), name="gated_deltanet")(
            p_gdn, p_gdn, p_gdn, p_gdn, p_gdn, p_gdn, p_gdn, conv_w, conv_w, conv_w, pad(a_log), pad(dt_bias),
            norm_g.reshape(1, LANES))


def _rwkv_prep_body(p_ref, halo_ref, mu_ref, w0_ref, wup_ref, a0_ref, aup_ref, kk_ref, ka_ref,
                    r_ref, lw_ref, k_ref, v_ref, kkn_ref, kka_ref):
    p = p_ref[0]
    first = pl.program_id(1) == 0
    last_prev = jnp.where(first, 0.0, halo_ref[0, 7:8, :])
    prev = jnp.where(_iota((p.shape[0], 1), 0) == 0, last_prev, pltpu.roll(p, 1, 0))
    xs = p + mu_ref[...] * (prev - p)
    r, k, v = xs[:, :GROUP], xs[:, GROUP:2 * GROUP], xs[:, 2 * GROUP:3 * GROUP]
    lora = xs[:, 3 * GROUP:]
    w = w0_ref[...] + jnp.dot(jnp.tanh(lora).astype(BF16), wup_ref[...], preferred_element_type=F32)
    a = jax.nn.sigmoid(a0_ref[...] + jnp.dot(lora.astype(BF16), aup_ref[...], preferred_element_type=F32))
    low = _iota((1, LANES), 1) < HEAD64
    r_ref[0] = r
    v_ref[0] = v
    lw_ref[0] = -jnp.exp(-_softplus(-w) - 0.5)
    k_ref[0] = k * (1.0 + (a - 1.0) * ka_ref[...])
    for blk in range(GROUP // LANES):
        sl = slice(blk * LANES, (blk + 1) * LANES)
        kk0 = k[:, sl] * kk_ref[:, sl]
        kkn = kk0 * lax.rsqrt(_half_sum(kk0 * kk0, low) + 1e-6)
        kkn_ref[0, :, sl] = kkn
        kka_ref[0, :, sl] = kkn * a[:, sl]


def _rwkv_body(r_ref, lw_ref, k_ref, v_ref, kk_ref, kka_ref, rk_ref, lng_ref, lnb_ref, o_ref,
               ra_s, y1_s, p_s, q_s, gam_s, state_s, *, ts, npairs, unroll):
    low = _iota((1, LANES), 1) < HEAD64
    m_tril, m_strict, _, _ = _block_masks()
    tril_b = (_iota((CHUNK, CHUNK), 0) >= _iota((CHUNK, CHUNK), 1)).astype(BF16)
    split = lambda x: jnp.concatenate([jnp.where(low, x, 0.0), jnp.where(low, 0.0, x)], axis=0)
    unsplit = lambda x: x[:CHUNK] + x[CHUNK:]
    mul = lambda a, b: a * b

    @pl.when(pl.program_id(2) == 0)
    def _():
        state_s[...] = jnp.zeros_like(state_s)

    def prepare(it, carry):
        chains = [(pr, it * unroll + u) for u in range(unroll) for pr in range(npairs)]
        rows = [pl.ds(pl.multiple_of(c * CHUNK, CHUNK), CHUNK) for _, c in chains]
        lanes = [slice(pr * LANES, (pr + 1) * LANES) for pr, _ in chains]
        ld = lambda ref: [ref[0, rw, ln] for rw, ln in zip(rows, lanes)]
        lw = ld(lw_ref)
        lw_hi = _each(lambda x: x.astype(BF16), lw)
        lw_lo = _each(lambda x, hi: (x - hi.astype(F32)).astype(BF16), lw, lw_hi)
        cum = _each(lambda hi, lo: (jnp.dot(tril_b, hi, preferred_element_type=F32)
                                    + jnp.dot(tril_b, lo, preferred_element_type=F32)), lw_hi, lw_lo)
        gam = _each(jnp.exp, cum)
        inv = _each(lambda c: jnp.exp(-c), cum)
        g_end = _each(lambda g: g[CHUNK - 1:CHUNK, :], gam)
        at = _each(lambda kk, c, w: -kk * jnp.exp(c - w), ld(kk_ref), cum, lw)
        rt = _each(mul, ld(r_ref), gam)
        bt = _each(mul, ld(kka_ref), inv)
        kt = _each(mul, ld(k_ref), inv)
        a_st = _each(split, at)
        v_st = _each(split, ld(v_ref))
        gram = _each(lambda a, r, b, k: _mm_nt(jnp.concatenate([a, split(r)], axis=0),
                                               jnp.concatenate([b, b, k, k], axis=0)), a_st, rt, bt, kt)
        n_ab = _each(lambda g: jnp.where(m_strict, g[:LANES, :LANES], 0.0), gram)
        n_ak = _each(lambda g: jnp.where(m_strict, g[:LANES, LANES:], 0.0), gram)
        n_rb = _each(lambda g: jnp.where(m_tril, g[LANES:, :LANES], 0.0), gram)
        n_rk = _each(lambda g: jnp.where(m_tril, g[LANES:, LANES:], 0.0), gram)
        t = _neumann_inverse(n_ab)
        akv = _each(_mm, n_ak, v_st)
        z = _each(lambda t_, a, x: _mm(t_, jnp.concatenate([a, x], axis=1)), t, a_st, akv)
        rz = _each(_mm, n_rb, z)
        rkv = _each(_mm, n_rk, v_st)
        b_st = _each(lambda b, g: split(b * g), bt, g_end)
        k_st = _each(lambda k, g: split(k * g), kt, g_end)
        pm = _each(lambda b, z_: _mm_tn(b, z_[:, :LANES]), b_st, z)
        qm = _each(lambda b, k, z_, v: _mm_tn(jnp.concatenate([b, k], axis=0),
                                              jnp.concatenate([z_[:, LANES:], v], axis=0)), b_st, k_st, z, v_st)
        for i, (pr, c) in enumerate(chains):
            sq = pl.ds(pl.multiple_of(c * LANES, LANES), LANES)
            ra_s[pr, rows[i], :] = (rt[i] + unsplit(rz[i][:, :LANES])).astype(BF16)
            y1_s[pr, rows[i], :] = unsplit(rkv[i] + rz[i][:, LANES:])
            p_s[pr, sq, :] = pm[i].astype(BF16)
            q_s[pr, sq, :] = qm[i]
            gam_s[pr, sq, :] = jnp.broadcast_to(g_end[i], (LANES, LANES)).T
        return carry

    lax.fori_loop(0, ts // CHUNK // unroll, prepare, 0)

    def recur(c, hs):
        rows = pl.ds(pl.multiple_of(c * CHUNK, CHUNK), CHUNK)
        sq = pl.ds(pl.multiple_of(c * LANES, LANES), LANES)
        hb = [h.astype(BF16) for h in hs]
        for pr in range(npairs):
            y1_s[pr, rows, :] = jnp.dot(ra_s[pr, rows, :], hb[pr], preferred_element_type=F32) + y1_s[pr, rows, :]
        return tuple(gam_s[pr, sq, :] * hs[pr] + jnp.dot(p_s[pr, sq, :], hb[pr], preferred_element_type=F32)
                     + q_s[pr, sq, :] for pr in range(npairs))

    final = lax.fori_loop(0, ts // CHUNK, recur, tuple(state_s[pr] for pr in range(npairs)))
    for pr in range(npairs):
        state_s[pr] = final[pr]

    def finish(i, carry):
        rows = pl.ds(pl.multiple_of(i * 2 * CHUNK, 2 * CHUNK), 2 * CHUNK)
        lns = [slice(pr * LANES, (pr + 1) * LANES) for pr in range(npairs)]
        y = [y1_s[pr, rows, :] for pr in range(npairs)]
        rkk = [_half_sum(r_ref[0, rows, ln] * k_ref[0, rows, ln] * rk_ref[:, ln], low) for ln in lns]
        yc = _each(lambda y_: y_ - _half_sum(y_, low) * (1.0 / HEAD64), y)
        var = _each(lambda c: _half_sum(c * c, low) * (1.0 / HEAD64), yc)
        for pr, ln in enumerate(lns):
            out = (yc[pr] * lax.rsqrt(var[pr] + RWKV_GN_EPS) * lng_ref[:, ln] + lnb_ref[:, ln]
                   + rkk[pr] * v_ref[0, rows, ln])
            o_ref[0, rows, ln] = out.astype(o_ref.dtype)
        return carry

    lax.fori_loop(0, ts // (2 * CHUNK), finish, 0)


def _rwkv(p_rwkv, mu, w0, w_up, a0, a_up, k_k, k_a, r_k, ln_g, ln_b, tm=256, ts=1024, npairs=2, unroll=4):
    bsz, seq, _ = p_rwkv.shape
    tm, ts = min(tm, seq), min(ts, seq)
    unroll = min(unroll, ts // CHUNK)
    width = npairs * LANES
    nb = GROUP // width
    zeros = jnp.zeros((LORA, GROUP), F32)
    wup = jnp.concatenate([w_up, zeros], axis=0).astype(BF16)
    aup = jnp.concatenate([zeros, a_up], axis=0).astype(BF16)
    row = lambda v: v.reshape(1, -1)
    full = lambda n: pl.BlockSpec((1, n), lambda b, i: (0, 0))
    out_spec = pl.BlockSpec((1, tm, GROUP), lambda b, i: (b, i, 0))
    seq_f32 = jax.ShapeDtypeStruct((bsz, seq, GROUP), F32)
    r, lw, k, v, kk, kka = pl.pallas_call(
        _rwkv_prep_body, grid=(bsz, seq // tm),
        in_specs=[pl.BlockSpec((1, tm, N_RWKV), lambda b, i: (b, i, 0)),
                  pl.BlockSpec((1, 8, N_RWKV), lambda b, i: (b, jnp.maximum(i * (tm // 8) - 1, 0), 0)),
                  full(N_RWKV), full(GROUP),
                  pl.BlockSpec((2 * LORA, GROUP), lambda b, i: (0, 0)), full(GROUP),
                  pl.BlockSpec((2 * LORA, GROUP), lambda b, i: (0, 0)), full(GROUP), full(GROUP)],
        out_specs=[out_spec] * 6, out_shape=[seq_f32] * 6, compiler_params=_params(2),
        name="rwkv_prep")(p_rwkv, p_rwkv, row(mu), row(w0), wup, row(a0), aup, row(k_k), row(k_a))
    n_chunks = ts // CHUNK
    seq_spec = pl.BlockSpec((1, ts, width), lambda b, h, s: (b, s, h))
    par_spec = pl.BlockSpec((1, width), lambda b, h, s: (0, h))
    return pl.pallas_call(
        functools.partial(_rwkv_body, ts=ts, npairs=npairs, unroll=unroll), grid=(bsz, nb, seq // ts),
        in_specs=[seq_spec] * 6 + [par_spec] * 3,
        out_specs=seq_spec, out_shape=jax.ShapeDtypeStruct((bsz, seq, GROUP), BF16),
        scratch_shapes=[pltpu.VMEM((npairs, ts, LANES), BF16), pltpu.VMEM((npairs, ts, LANES), F32),
                        pltpu.VMEM((npairs, n_chunks * LANES, LANES), BF16),
                        pltpu.VMEM((npairs, n_chunks * LANES, LANES), F32),
                        pltpu.VMEM((npairs, n_chunks * LANES, LANES), F32),
                        pltpu.VMEM((npairs, LANES, LANES), F32)],
        compiler_params=_params(3), name="rwkv7")(
            r, lw, k, v, kk, kka, row(r_k), row(ln_g), row(ln_b))


def _split_w_in(w_in):
    b0, b1, b2, b3 = N_RWKV, N_RWKV + N_FOX, N_RWKV + N_FOX + N_GDN, N_RWKV + N_FOX + N_GDN + N_DIFF
    pad = lambda w, n: jnp.pad(w, ((0, 0), (0, 0), (0, n - w.shape[-1])))
    cols = lambda lo, hi: w_in[..., lo:hi].astype(BF16)
    w_a = jnp.concatenate([cols(0, b0), pad(cols(b0, b1), SEG_PAD)], axis=-1)
    w_b = jnp.concatenate([pad(cols(b1, b2), GDN_PAD), cols(b2, b3)], axis=-1)
    return w_a, w_b, cols(b3, w_in.shape[-1])


def kernel(x, norm_g, w_in, w_out, rwkv_mu, rwkv_w0, rwkv_w_up, rwkv_a0, rwkv_a_up, rwkv_k_k, rwkv_k_a, rwkv_r_k, rwkv_ln_g, rwkv_ln_b, fox_q_g, fox_k_g, fox_f_b, gdn_conv, gdn_a_log, gdn_dt_bias, gdn_norm_g, diff_q_g, diff_k_g, diff_lq1, diff_lk1, diff_lq2, diff_lk2, diff_subln_g):
    bsz, seq, d = x.shape
    m = bsz * seq
    w_a, w_b, w_z = _split_w_in(w_in)
    w_out_b = w_out.astype(BF16)
    x2d = x.reshape(m, d)
    for l in range(w_in.shape[0]):
        h = _rmsnorm(x2d, norm_g[l])
        p_a = _matmul(h, w_a[l], IN_PROJ_TN, F32).reshape(bsz, seq, -1)
        p_b = _matmul(h, w_b[l], IN_PROJ_TN, F32).reshape(bsz, seq, -1)
        sz = _matmul(h, w_z[l], 1024, BF16, silu=True, name="in_proj_gate")
        y_rwkv = _rwkv(p_a, rwkv_mu[l], rwkv_w0[l], rwkv_w_up[l], rwkv_a0[l], rwkv_a_up[l], rwkv_k_k[l],
                       rwkv_k_a[l], rwkv_r_k[l].reshape(-1), rwkv_ln_g[l], rwkv_ln_b[l])
        y_fox = _fox(p_a, fox_f_b[l], fox_q_g[l], fox_k_g[l], col0=N_RWKV)
        y_gdn = _gdn(p_b, gdn_conv[l], gdn_a_log[l], gdn_dt_bias[l], gdn_norm_g[l])
        y_diff = _diff(p_b, l, diff_q_g[l], diff_k_g[l], diff_lq1[l], diff_lk1[l], diff_lq2[l], diff_lk2[l],
                       diff_subln_g[l], col0=GDN_PAD)
        ys = [y.reshape(m, GROUP) for y in (y_rwkv, y_fox, y_gdn, y_diff)]
        x2d = _outproj(x2d, ys, sz, w_out_b[l])
    return x2d.reshape(bsz, seq, d)
```

```python
import functools
import math

import jax
import jax.numpy as jnp
from jax import lax
from jax.experimental import pallas as pl
from jax.experimental.pallas import tpu as pltpu

F32 = jnp.float32
BF16 = jnp.bfloat16
HI = lax.Precision.HIGHEST

D_MODEL = 2048
DEPTH = 4
GROUP = 1024
D_MIX = 4 * GROUP
LANES = 128
HEAD64 = 64
GDN_HEAD = 128
CHUNK = 64
LORA = 64
RWKV_GN_EPS = 64e-5
NORM_EPS = 1e-6
GDN_CONV = 4
N_RWKV = 3 * GROUP + 2 * LORA
N_FOX = 3 * GROUP + 16
N_GDN = 3 * GROUP + 16
N_DIFF = 3 * GROUP
SEG_PAD = 3200
GDN_PAD = 3328
IN_PROJ_TN = 1280
VMEM_LIMIT = 56 * 1024 * 1024

NT = (((1,), (1,)), ((), ()))
TN = (((0,), (0,)), ((), ()))


def _params(n_axes):
    return pltpu.CompilerParams(dimension_semantics=("arbitrary",) * n_axes,
                                vmem_limit_bytes=VMEM_LIMIT)


def _softplus(x):
    return jnp.maximum(x, 0.0) + jnp.log1p(jnp.exp(-jnp.abs(x)))


def _iota(shape, axis):
    return lax.broadcasted_iota(jnp.int32, shape, axis)


def _each(fn, *lists):
    return [fn(*args) for args in zip(*lists)]


def _half_sum(x, low):
    s0 = jnp.sum(jnp.where(low, x, 0.0), axis=1, keepdims=True)
    s1 = jnp.sum(jnp.where(low, 0.0, x), axis=1, keepdims=True)
    return jnp.where(low, s0, s1)


def _mm(a, b):
    return jnp.dot(a.astype(BF16), b.astype(BF16), preferred_element_type=F32)


def _mm_nt(a, b):
    return lax.dot_general(a.astype(BF16), b.astype(BF16), NT, preferred_element_type=F32)


def _mm_tn(a, b):
    return lax.dot_general(a.astype(BF16), b.astype(BF16), TN, preferred_element_type=F32)


def _neumann_inverse(ns):
    eye = (_iota((LANES, LANES), 0) == _iota((LANES, LANES), 1)).astype(F32)

    ts = [eye + n for n in ns]
    ps = _each(_mm, ns, ns)
    for _ in range(4):
        tp = _each(lambda t, p: _mm(jnp.concatenate([t, p], axis=0), p), ts, ps)
        ts = _each(lambda t, x: t + x[:LANES], ts, tp)
        ps = _each(lambda x: x[LANES:], tp)
    return _each(lambda t, p: t + _mm(t, p), ts, ps)


def _block_masks():
    ri = _iota((LANES, LANES), 0)
    ci = _iota((LANES, LANES), 1)
    same = (ri < CHUNK) == (ci < CHUNK)
    return same & (ri >= ci), same & (ri > ci), same & (ri <= ci), same


def _rmsnorm_body(x_ref, g_ref, o_ref):
    x = x_ref[...]
    ms = jnp.mean(x * x, axis=-1, keepdims=True)
    o_ref[...] = (x * lax.rsqrt(ms + NORM_EPS) * g_ref[...]).astype(o_ref.dtype)


def _rmsnorm(x2d, g, tm=512):
    m, d = x2d.shape
    return pl.pallas_call(
        _rmsnorm_body, grid=(m // tm,),
        in_specs=[pl.BlockSpec((tm, d), lambda i: (i, 0)), pl.BlockSpec((1, d), lambda i: (0, 0))],
        out_specs=pl.BlockSpec((tm, d), lambda i: (i, 0)),
        out_shape=jax.ShapeDtypeStruct((m, d), BF16), compiler_params=_params(1),
        name="rmsnorm")(x2d, g.reshape(1, d))


def _matmul_body(h_ref, w_ref, o_ref, *, silu):
    acc = jnp.dot(h_ref[...], w_ref[...], preferred_element_type=F32)
    if silu:
        acc = acc * jax.nn.sigmoid(acc)
    o_ref[...] = acc.astype(o_ref.dtype)


def _matmul(h, w, tn, out_dtype, silu=False, tm=1024, name="in_proj"):
    m, k = h.shape
    n = w.shape[1]
    tm = min(tm, m)
    return pl.pallas_call(
        functools.partial(_matmul_body, silu=silu), grid=(m // tm, n // tn),
        in_specs=[pl.BlockSpec((tm, k), lambda i, j: (i, 0)), pl.BlockSpec((k, tn), lambda i, j: (0, j))],
        out_specs=pl.BlockSpec((tm, tn), lambda i, j: (i, j)),
        out_shape=jax.ShapeDtypeStruct((m, n), out_dtype), compiler_params=_params(2),
        name=name)(h, w)


def _outproj_body(x_ref, yr_ref, yf_ref, yg_ref, yd_ref, sz_ref, w_ref, o_ref):
    acc = x_ref[...]
    for g, y_ref in enumerate((yr_ref, yf_ref, yg_ref, yd_ref)):
        gate = (y_ref[...].astype(F32) * sz_ref[:, g * GROUP:(g + 1) * GROUP].astype(F32)).astype(BF16)
        acc = acc + jnp.dot(gate, w_ref[g * GROUP:(g + 1) * GROUP, :], preferred_element_type=F32)
    o_ref[...] = acc


def _outproj(x2d, ys, sz, w_out, tm=512, tn=1024):
    m, d = x2d.shape
    tm = min(tm, m)
    yspec = pl.BlockSpec((tm, GROUP), lambda j, i: (i, 0))
    return pl.pallas_call(
        _outproj_body, grid=(d // tn, m // tm),
        in_specs=[pl.BlockSpec((tm, tn), lambda j, i: (i, j)), yspec, yspec, yspec, yspec,
                  pl.BlockSpec((tm, D_MIX), lambda j, i: (i, 0)),
                  pl.BlockSpec((D_MIX, tn), lambda j, i: (0, j))],
        out_specs=pl.BlockSpec((tm, tn), lambda j, i: (i, j)),
        out_shape=jax.ShapeDtypeStruct((m, d), F32), compiler_params=_params(2),
        name="out_proj")(x2d, *ys, sz, w_out)


def _aug_lanes(h):
    first = (1 - h) * HEAD64
    lane = _iota((1, LANES), 1)
    return lane == first, lane == first + 1


ONES_ROWS = 16
LOG2E = 1.4426950408889634


def _attend(qts, kn_ref, vt_ref, sa_ref, sb_ref, acc_ref, sigma_fns, qi, tq, tk):
    q_pos = qi * tq + _iota((1, tq), 1)
    heads = range(len(qts))

    def produce(j, s_ref, diagonal=False):
        k0 = pl.multiple_of(j * tk, tk)
        s = [jnp.dot(kn_ref[h, pl.ds(k0, tk), :], qts[h], preferred_element_type=F32) for h in heads]
        if diagonal:
            visible = (k0 + _iota((tk, 1), 0)) <= q_pos
            s = [jnp.where(visible, x, -jnp.inf) for x in s]
        for h in heads:
            s_ref[h] = s[h]
        return [jnp.max(x, axis=0, keepdims=True) for x in s]

    def consume(j, s_ref, s_max, m):
        sigma = [sigma_fns[h](j) * LOG2E for h in heads]
        m_new = [jnp.maximum(m[h], s_max[h] + sigma[h]) for h in heads]
        alpha = [jnp.exp2(m[h] - m_new[h]) for h in heads]
        p = [jnp.exp2(s_ref[h] - (m_new[h] - sigma[h])).astype(BF16) for h in heads]
        pv = [jnp.dot(vt_ref[h % vt_ref.shape[0], j], p[h], preferred_element_type=F32) for h in heads]
        for h in heads:
            acc_ref[h] = alpha[h] * acc_ref[h] + pv[h]
        return m_new

    n_full = (qi * tq) // tk
    acc_ref[...] = jnp.zeros_like(acc_ref)
    max_a = produce(n_full, sa_ref, diagonal=True)

    def pair(t, carry):
        m, max_a, in_a = carry
        max_b = produce(2 * t, sb_ref)
        m = consume(in_a, sa_ref, max_a, m)
        max_a = produce(2 * t + 1, sa_ref)
        m = consume(2 * t, sb_ref, max_b, m)
        return m, max_a, 2 * t + 1

    m, max_a, in_a = lax.fori_loop(0, n_full // 2, pair,
                                   ([jnp.full((1, tq), -jnp.inf, F32)] * 2, max_a, n_full))

    @pl.when(n_full % 2 == 1)
    def _():
        max_b = produce(n_full - 1, sb_ref)
        consume(n_full - 1, sb_ref, max_b, consume(in_a, sa_ref, max_a, m))

    @pl.when(n_full % 2 == 0)
    def _():
        consume(in_a, sa_ref, max_a, m)


def _half_rmsnorm(x, gain, low):
    ms = _half_sum(x * x, low) * (1.0 / HEAD64)
    return x * lax.rsqrt(ms + NORM_EPS) * gain


def _augment_queries(qn):
    qt = qn.T
    row = _iota((LANES, 1), 0)
    out = []
    for h in range(2):
        first = (1 - h) * HEAD64
        own = row < HEAD64 if h == 0 else row >= HEAD64
        out.append(jnp.where(own, qt, jnp.where((row == first) | (row == first + 1), 1.0, 0.0)).astype(BF16))
    return out


def _normalised_output(acc):
    dv = acc.shape[0] - ONES_ROWS
    return acc[:dv] / acc[dv:dv + 1]


def _stage_keys(k_ref, v_ref, kg_ref, kn_s, vt_s, low, seq, tk, key_bias):
    def body(i, c):
        r0 = pl.multiple_of(i * tk, tk)
        rows = pl.ds(r0, tk)
        kn = _half_rmsnorm(k_ref[0, rows, :], kg_ref[...], low)
        for h in range(2):
            d = key_bias(h, r0) * LOG2E
            d_hi = d.astype(BF16).astype(F32)
            a0, a1 = _aug_lanes(h)
            aug = jnp.where(a0, d_hi, jnp.where(a1, d - d_hi, 0.0))
            kn_s[h, rows, :] = jnp.where(low if h == 0 else ~low, kn, aug).astype(BF16)
        vt = v_ref[0, rows, :].T.astype(BF16)
        dv = vt_s.shape[2] - ONES_ROWS
        for g in range(vt_s.shape[0]):
            vt_s[g, i, :dv, :] = vt[g * dv:(g + 1) * dv]
            vt_s[g, i, dv:, :] = jnp.ones((ONES_ROWS, tk), BF16)
        return c

    lax.fori_loop(0, seq // tk, body, 0)


def _fox_body(q_ref, k_ref, v_ref, ck_ref, cq_ref, cc_ref, qg_ref, kg_ref, o_ref, kn_s, vb_s, sa_s, sb_s, acc_s,
              *, seq, tq, tk):
    hp = pl.program_id(1)
    qi = pl.program_id(2)
    low = _iota((1, LANES), 1) < HEAD64

    def key_bias(h, r0):
        c_blk = jnp.sum(jnp.where(_iota((1, 16), 1) == 2 * hp + h, cc_ref[0, pl.ds(r0, tk), :], 0.0),
                        axis=1, keepdims=True)
        return c_blk[0:1, :] - c_blk

    @pl.when(qi == 0)
    def _():
        _stage_keys(k_ref, v_ref, kg_ref, kn_s, vb_s, low, seq, tk, key_bias)

    qn = _half_rmsnorm(q_ref[0], qg_ref[...], low) * (HEAD64 ** -0.5 * LOG2E)
    sigma_fns = []
    for h in range(2):
        row = 2 * hp + h
        c_q0 = cq_ref[0, row, pl.ds(qi, 1), :][:, 0:1]
        sigma_fns.append(lambda j, row=row, c_q0=c_q0: c_q0 - ck_ref[0, row, pl.ds(j, 1), :][:, 0:1])
    _attend(_augment_queries(qn), kn_s, vb_s, sa_s, sb_s, acc_s, sigma_fns, qi, tq, tk)
    o_ref[0] = jnp.concatenate([_normalised_output(acc_s[0]), _normalised_output(acc_s[1])],
                               axis=0).T.astype(o_ref.dtype)


def _fox_cum_body(f_ref, fb_ref, c_ref, *, seq):
    x = f_ref[0] + fb_ref[...]
    logf = jnp.minimum(x, 0.0) - jnp.log1p(jnp.exp(-jnp.abs(x)))
    upper = (_iota((LANES, LANES), 0) <= _iota((LANES, LANES), 1)).astype(F32)
    carry = jnp.zeros((16, 1), F32)
    for blk in range(seq // LANES):
        sl = slice(blk * LANES, (blk + 1) * LANES)
        cb = jnp.dot(logf[:, sl], upper, precision=HI, preferred_element_type=F32) + carry
        c_ref[0, :, sl] = cb
        carry = cb[:, LANES - 1:LANES]


def _attention_scratch(seq, tq, tk, dv):
    return [pltpu.VMEM((2, seq, LANES), BF16), pltpu.VMEM((LANES // dv, seq // tk, dv + ONES_ROWS, tk), BF16),
            pltpu.VMEM((2, tk, tq), F32), pltpu.VMEM((2, tk, tq), F32),
            pltpu.VMEM((2, dv + ONES_ROWS, tq), F32)]


def _fox(p_fox, f_b, q_g, k_g, col0=0, tq=512, tk=512):
    bsz, seq, _ = p_fox.shape
    tq, tk = min(tq, seq), min(tk, seq)
    cb = col0 // LANES
    f_t = jnp.transpose(p_fox[:, :, col0 + 3 * GROUP:col0 + 3 * GROUP + 16], (0, 2, 1))
    c = pl.pallas_call(
        functools.partial(_fox_cum_body, seq=seq), grid=(bsz,),
        in_specs=[pl.BlockSpec((1, 16, seq), lambda b: (b, 0, 0)), pl.BlockSpec((16, 1), lambda b: (0, 0))],
        out_specs=pl.BlockSpec((1, 16, seq), lambda b: (b, 0, 0)),
        out_shape=jax.ShapeDtypeStruct((bsz, 16, seq), F32), compiler_params=_params(1),
        name="fox_cumsum")(f_t, f_b.reshape(16, 1))
    nb = GROUP // LANES
    gain = lambda g: jnp.tile(g, 2).reshape(1, LANES)
    return pl.pallas_call(
        functools.partial(_fox_body, seq=seq, tq=tq, tk=tk), grid=(bsz, nb, seq // tq),
        in_specs=[pl.BlockSpec((1, tq, LANES), lambda b, h, i: (b, i, cb + h)),
                  pl.BlockSpec((1, seq, LANES), lambda b, h, i: (b, 0, cb + nb + h)),
                  pl.BlockSpec((1, seq, LANES), lambda b, h, i: (b, 0, cb + 2 * nb + h)),
                  pl.BlockSpec((1, 16, seq // tk, tk), lambda b, h, i: (b, 0, 0, 0)),
                  pl.BlockSpec((1, 16, seq // tq, tq), lambda b, h, i: (b, 0, 0, 0)),
                  pl.BlockSpec((1, seq, 16), lambda b, h, i: (b, 0, 0)),
                  pl.BlockSpec((1, LANES), lambda b, h, i: (0, 0)),
                  pl.BlockSpec((1, LANES), lambda b, h, i: (0, 0))],
        out_specs=pl.BlockSpec((1, tq, LANES), lambda b, h, i: (b, i, h)),
        out_shape=jax.ShapeDtypeStruct((bsz, seq, GROUP), BF16),
        scratch_shapes=_attention_scratch(seq, tq, tk, HEAD64),
        compiler_params=_params(3), name="fox_attention")(
            p_fox, p_fox, p_fox, c.reshape(bsz, 16, seq // tk, tk), c.reshape(bsz, 16, seq // tq, tq),
            jnp.transpose(c, (0, 2, 1)), gain(q_g), gain(k_g))


def _diff_body(q_ref, k_ref, v_ref, qg_ref, kg_ref, lam_ref, sg_ref, o_ref, kn_s, vb_s, sa_s, sb_s, acc_s,
               *, seq, tq, tk, lam_init):
    head = pl.program_id(1)
    qi = pl.program_id(2)
    low = _iota((1, LANES), 1) < HEAD64
    slope = jnp.exp2(-(head + 1).astype(F32) * jnp.ones((1, 1), F32))

    @pl.when(qi == 0)
    def _():
        in_block = _iota((tk, 1), 0).astype(F32)
        _stage_keys(k_ref, v_ref, kg_ref, kn_s, vb_s, low, seq, tk, lambda h, r0: slope * in_block)

    lq1, lk1, lq2, lk2 = (lam_ref[i:i + 1, :] for i in range(4))
    lam = (jnp.exp(jnp.sum(lq1 * lk1, axis=1, keepdims=True))
           - jnp.exp(jnp.sum(lq2 * lk2, axis=1, keepdims=True)) + lam_init)

    def sigma(j):
        return slope * (j * tk - qi * tq).astype(F32)

    qn = _half_rmsnorm(q_ref[0], qg_ref[...], low) * (HEAD64 ** -0.5 * LOG2E)
    _attend(_augment_queries(qn), kn_s, vb_s, sa_s, sb_s, acc_s, [sigma, sigma], qi, tq, tk)
    o = (_normalised_output(acc_s[0]) - lam * _normalised_output(acc_s[1])).T
    ms = jnp.mean(o * o, axis=1, keepdims=True)
    o_ref[0] = (o * lax.rsqrt(ms + 1e-5) * sg_ref[...] * (1.0 - lam_init)).astype(o_ref.dtype)


def _diff(p_diff, layer, q_g, k_g, lq1, lk1, lq2, lk2, subln_g, col0=0, tq=512, tk=512):
    bsz, seq, _ = p_diff.shape
    tq, tk = min(tq, seq), min(tk, seq)
    nb = GROUP // LANES
    cb = col0 // LANES
    lam_init = 0.8 - 0.6 * math.exp(-0.3 * layer)
    gain = lambda g: jnp.tile(g, 2).reshape(1, LANES)
    lam_rows = jnp.stack([lq1, lk1, lq2, lk2])
    return pl.pallas_call(
        functools.partial(_diff_body, seq=seq, tq=tq, tk=tk, lam_init=lam_init), grid=(bsz, nb, seq // tq),
        in_specs=[pl.BlockSpec((1, tq, LANES), lambda b, h, i: (b, i, cb + h)),
                  pl.BlockSpec((1, seq, LANES), lambda b, h, i: (b, 0, cb + nb + h)),
                  pl.BlockSpec((1, seq, LANES), lambda b, h, i: (b, 0, cb + 2 * nb + h)),
                  pl.BlockSpec((1, LANES), lambda b, h, i: (0, 0)),
                  pl.BlockSpec((1, LANES), lambda b, h, i: (0, 0)),
                  pl.BlockSpec((4, HEAD64), lambda b, h, i: (0, 0)),
                  pl.BlockSpec((1, LANES), lambda b, h, i: (0, 0))],
        out_specs=pl.BlockSpec((1, tq, LANES), lambda b, h, i: (b, i, h)),
        out_shape=jax.ShapeDtypeStruct((bsz, seq, GROUP), BF16),
        scratch_shapes=_attention_scratch(seq, tq, tk, 2 * HEAD64),
        compiler_params=_params(3), name="diff_attention")(
            p_diff, p_diff, p_diff, gain(q_g), gain(k_g), lam_rows, subln_g.reshape(1, LANES))


def _gdn_body(q_ref, k_ref, v_ref, qh_ref, kh_ref, vh_ref, gate_ref, cwq_ref, cwk_ref, cwv_ref, alog_ref, dt_ref,
              ng_ref, o_ref, qw_s, au_s, p_s, q_s, eg_s, state_s, *, ts, nheads, unroll):
    pair = 2 * CHUNK
    first_tile = pl.program_id(2) == 0
    lane = _iota((1, LANES), 1)
    pick = lambda x, idx: jnp.sum(jnp.where(lane == idx, x, 0.0), axis=1, keepdims=True)
    m_tril, m_strict, m_upper, m_same = _block_masks()
    eye = _iota((pair, pair), 0) == _iota((pair, pair), 1)
    head_ids = [pl.program_id(1) * nheads + hd for hd in range(nheads)]
    a_scale = [-jnp.exp(pick(alog_ref[...], h)) for h in head_ids]
    dt_bias = [pick(dt_ref[...], h) for h in head_ids]

    @pl.when(first_tile)
    def _():
        state_s[...] = jnp.zeros_like(state_s)

    def prepare(it, carry):
        chains = [(hd, it * unroll + u) for u in range(unroll) for hd in range(nheads)]
        lanes = [slice(hd * LANES, (hd + 1) * LANES) for hd, _ in chains]
        r0s = [pl.multiple_of(cp * pair, pair) for _, cp in chains]

        def conv_silu(ref, halo_ref, cw_ref):
            def one(ln, r0, chain):
                cur = ref[0, pl.ds(r0, pair), ln]
                inside = ref[0, pl.ds(pl.multiple_of(jnp.maximum(r0 - 8, 0), 8), 8), ln]
                before = jnp.where(first_tile, 0.0, halo_ref[0, :, ln])
                ext = jnp.concatenate([jnp.where(chain[1] == 0, before, inside), cur], axis=0)
                acc = cur * cw_ref[GDN_CONV - 1:GDN_CONV, ln]
                for j in range(1, GDN_CONV):
                    acc = acc + pltpu.roll(ext, j, 0)[8:, :] * cw_ref[GDN_CONV - 1 - j:GDN_CONV - j, ln]
                return acc * jax.nn.sigmoid(acc)
            return _each(one, lanes, r0s, chains)

        q = conv_silu(q_ref, qh_ref, cwq_ref)
        k = conv_silu(k_ref, kh_ref, cwk_ref)
        v = conv_silu(v_ref, vh_ref, cwv_ref)
        q = _each(lambda x: x * lax.rsqrt(jnp.sum(x * x, axis=1, keepdims=True) + 1e-6) * (GDN_HEAD ** -0.5), q)
        k = _each(lambda x: x * lax.rsqrt(jnp.sum(x * x, axis=1, keepdims=True) + 1e-6), k)
        gates = [gate_ref[0, pl.ds(r0, pair), :] for r0 in r0s]
        g_col = [a_scale[hd] * _softplus(pick(g, head_ids[hd]) + dt_bias[hd]) for (hd, _), g in zip(chains, gates)]
        beta = [jax.nn.sigmoid(pick(g, head_ids[hd] + 8)) for (hd, _), g in zip(chains, gates)]
        g_row = _each(lambda g: jnp.sum(jnp.where(eye, g, 0.0), axis=0, keepdims=True), g_col)
        gc_col = _each(lambda g: jnp.sum(jnp.where(m_tril, g, 0.0), axis=1, keepdims=True), g_row)
        gc_row = _each(lambda g: jnp.sum(jnp.where(m_upper, g, 0.0), axis=0, keepdims=True), g_col)
        g_last = _each(lambda g: jnp.sum(jnp.where(m_same, g, 0.0), axis=1, keepdims=True), g_row)
        decay = _each(lambda c, r: jnp.exp(jnp.where(m_tril, c - r, -jnp.inf)), gc_col, gc_row)
        exp_gc = _each(jnp.exp, gc_col)
        kb = _each(lambda x, b: x * b, k, beta)
        kk = _each(_mm_nt, kb, k)
        t = _neumann_inverse(_each(lambda x, d: -jnp.where(m_strict, x * d, 0.0), kk, decay))
        uw = _each(lambda t_, v_, b, kb_, e: _mm(t_, jnp.concatenate([v_ * b, kb_ * e], axis=1)),
                   t, v, beta, kb, exp_gc)
        ai = _each(lambda q_, k_, d: _mm_nt(q_, k_) * d, q, k, decay)
        aiuw = _each(_mm, ai, uw)
        kd = _each(lambda k_, gl, gc: k_ * jnp.exp(gl - gc), k, g_last, gc_col)
        kuw = [[_mm_tn(kd_[half * CHUNK:(half + 1) * CHUNK], uw_[half * CHUNK:(half + 1) * CHUNK])
                for kd_, uw_ in zip(kd, uw)] for half in range(2)]
        for i, (hd, cp) in enumerate(chains):
            rows = pl.ds(r0s[i], pair)
            qw_s[hd, rows, :] = (q[i] * exp_gc[i] - aiuw[i][:, LANES:]).astype(BF16)
            au_s[hd, rows, :] = aiuw[i][:, :LANES]
            e_last = jnp.exp(g_last[i])
            for half in range(2):
                c = 2 * cp + half
                sq = pl.ds(pl.multiple_of(c * LANES, LANES), LANES)
                p_s[hd, sq, :] = (-kuw[half][i][:, LANES:]).astype(BF16)
                q_s[hd, sq, :] = kuw[half][i][:, :LANES]
                eg_s[hd, pl.ds(c, 1), :] = jnp.broadcast_to(e_last[half * CHUNK:half * CHUNK + 1], (1, LANES))
        return carry

    lax.fori_loop(0, ts // pair // unroll, prepare, 0)

    def recur(c, states):
        rows = pl.ds(pl.multiple_of(c * CHUNK, CHUNK), CHUNK)
        sq = pl.ds(pl.multiple_of(c * LANES, LANES), LANES)
        sb = [s.astype(BF16) for s in states]
        for hd in range(nheads):
            au_s[hd, rows, :] = jnp.dot(qw_s[hd, rows, :], sb[hd], preferred_element_type=F32) + au_s[hd, rows, :]
        return tuple(states[hd] * eg_s[hd, pl.ds(c, 1), :]
                     + jnp.dot(p_s[hd, sq, :], sb[hd], preferred_element_type=F32) + q_s[hd, sq, :]
                     for hd in range(nheads))

    final = lax.fori_loop(0, ts // CHUNK, recur, tuple(state_s[hd] for hd in range(nheads)))
    for hd in range(nheads):
        state_s[hd] = final[hd]

    def finish(i, carry):
        rows = pl.ds(pl.multiple_of(i * 4 * CHUNK, 4 * CHUNK), 4 * CHUNK)
        for hd in range(nheads):
            o = au_s[hd, rows, :]
            ms = jnp.mean(o * o, axis=1, keepdims=True)
            o_ref[0, rows, hd * LANES:(hd + 1) * LANES] = (o * lax.rsqrt(ms + NORM_EPS) * ng_ref[...]).astype(o_ref.dtype)
        return carry

    lax.fori_loop(0, ts // (4 * CHUNK), finish, 0)


def _gdn(p_gdn, conv_w, a_log, dt_bias, norm_g, ts=1024, nheads=4, unroll=1):
    bsz, seq, _ = p_gdn.shape
    ts = min(ts, seq)
    nb = GROUP // LANES // nheads
    width = nheads * LANES
    n_chunks = ts // CHUNK
    pad = lambda v: jnp.pad(v, (0, LANES - v.shape[0])).reshape(1, LANES)
    seq_spec = lambda off: pl.BlockSpec((1, ts, width), lambda b, h, s: (b, s, off + h))
    halo_spec = lambda off: pl.BlockSpec((1, 8, width), lambda b, h, s: (b, jnp.maximum(s * (ts // 8) - 1, 0), off + h))
    cw_spec = lambda off: pl.BlockSpec((GDN_CONV, width), lambda b, h, s: (0, off + h))
    row_spec = pl.BlockSpec((1, LANES), lambda b, h, s: (0, 0))
    return pl.pallas_call(
        functools.partial(_gdn_body, ts=ts, nheads=nheads, unroll=unroll), grid=(bsz, nb, seq // ts),
        in_specs=[seq_spec(0), seq_spec(nb), seq_spec(2 * nb), halo_spec(0), halo_spec(nb), halo_spec(2 * nb),
                  pl.BlockSpec((1, ts, LANES), lambda b, h, s: (b, s, 3 * GROUP // LANES)),
                  cw_spec(0), cw_spec(nb), cw_spec(2 * nb), row_spec, row_spec, row_spec],
        out_specs=pl.BlockSpec((1, ts, width), lambda b, h, s: (b, s, h)),
        out_shape=jax.ShapeDtypeStruct((bsz, seq, GROUP), BF16),
        scratch_shapes=[pltpu.VMEM((nheads, ts, LANES), BF16), pltpu.VMEM((nheads, ts, LANES), F32),
                        pltpu.VMEM((nheads, n_chunks * LANES, LANES), BF16),
                        pltpu.VMEM((nheads, n_chunks * LANES, LANES), F32),
                        pltpu.VMEM((nheads, max(n_chunks, 8), LANES), F32),
                        pltpu.VMEM((nheads, GDN_HEAD, GDN_HEAD), F32)],
        compiler_params=_params(3), name="gated_deltanet")(
            p_gdn, p_gdn, p_gdn, p_gdn, p_gdn, p_gdn, p_gdn, conv_w, conv_w, conv_w, pad(a_log), pad(dt_bias),
            norm_g.reshape(1, LANES))


def _rwkv_prep_body(p_ref, halo_ref, mu_ref, w0_ref, wup_ref, a0_ref, aup_ref, kk_ref, ka_ref,
                    r_ref, lw_ref, k_ref, v_ref, kkn_ref, kka_ref):
    p = p_ref[0]
    first = pl.program_id(1) == 0
    last_prev = jnp.where(first, 0.0, halo_ref[0, 7:8, :])
    prev = jnp.where(_iota((p.shape[0], 1), 0) == 0, last_prev, pltpu.roll(p, 1, 0))
    xs = p + mu_ref[...] * (prev - p)
    r, k, v = xs[:, :GROUP], xs[:, GROUP:2 * GROUP], xs[:, 2 * GROUP:3 * GROUP]
    lora = xs[:, 3 * GROUP:]
    w = w0_ref[...] + jnp.dot(jnp.tanh(lora).astype(BF16), wup_ref[...], preferred_element_type=F32)
    a = jax.nn.sigmoid(a0_ref[...] + jnp.dot(lora.astype(BF16), aup_ref[...], preferred_element_type=F32))
    low = _iota((1, LANES), 1) < HEAD64
    r_ref[0] = r
    v_ref[0] = v
    lw_ref[0] = -jnp.exp(-_softplus(-w) - 0.5)
    k_ref[0] = k * (1.0 + (a - 1.0) * ka_ref[...])
    for blk in range(GROUP // LANES):
        sl = slice(blk * LANES, (blk + 1) * LANES)
        kk0 = k[:, sl] * kk_ref[:, sl]
        kkn = kk0 * lax.rsqrt(_half_sum(kk0 * kk0, low) + 1e-6)
        kkn_ref[0, :, sl] = kkn
        kka_ref[0, :, sl] = kkn * a[:, sl]


def _rwkv_body(r_ref, lw_ref, k_ref, v_ref, kk_ref, kka_ref, rk_ref, lng_ref, lnb_ref, o_ref,
               ra_s, y1_s, p_s, q_s, gam_s, state_s, *, ts, npairs, unroll):
    low = _iota((1, LANES), 1) < HEAD64
    m_tril, m_strict, _, _ = _block_masks()
    tril_b = (_iota((CHUNK, CHUNK), 0) >= _iota((CHUNK, CHUNK), 1)).astype(BF16)
    split = lambda x: jnp.concatenate([jnp.where(low, x, 0.0), jnp.where(low, 0.0, x)], axis=0)
    unsplit = lambda x: x[:CHUNK] + x[CHUNK:]
    mul = lambda a, b: a * b

    @pl.when(pl.program_id(2) == 0)
    def _():
        state_s[...] = jnp.zeros_like(state_s)

    def prepare(it, carry):
        chains = [(pr, it * unroll + u) for u in range(unroll) for pr in range(npairs)]
        rows = [pl.ds(pl.multiple_of(c * CHUNK, CHUNK), CHUNK) for _, c in chains]
        lanes = [slice(pr * LANES, (pr + 1) * LANES) for pr, _ in chains]
        ld = lambda ref: [ref[0, rw, ln] for rw, ln in zip(rows, lanes)]
        lw = ld(lw_ref)
        lw_hi = _each(lambda x: x.astype(BF16), lw)
        lw_lo = _each(lambda x, hi: (x - hi.astype(F32)).astype(BF16), lw, lw_hi)
        cum = _each(lambda hi, lo: (jnp.dot(tril_b, hi, preferred_element_type=F32)
                                    + jnp.dot(tril_b, lo, preferred_element_type=F32)), lw_hi, lw_lo)
        gam = _each(jnp.exp, cum)
        inv = _each(lambda c: jnp.exp(-c), cum)
        g_end = _each(lambda g: g[CHUNK - 1:CHUNK, :], gam)
        at = _each(lambda kk, c, w: -kk * jnp.exp(c - w), ld(kk_ref), cum, lw)
        rt = _each(mul, ld(r_ref), gam)
        bt = _each(mul, ld(kka_ref), inv)
        kt = _each(mul, ld(k_ref), inv)
        a_st = _each(split, at)
        v_st = _each(split, ld(v_ref))
        gram = _each(lambda a, r, b, k: _mm_nt(jnp.concatenate([a, split(r)], axis=0),
                                               jnp.concatenate([b, b, k, k], axis=0)), a_st, rt, bt, kt)
        n_ab = _each(lambda g: jnp.where(m_strict, g[:LANES, :LANES], 0.0), gram)
        n_ak = _each(lambda g: jnp.where(m_strict, g[:LANES, LANES:], 0.0), gram)
        n_rb = _each(lambda g: jnp.where(m_tril, g[LANES:, :LANES], 0.0), gram)
        n_rk = _each(lambda g: jnp.where(m_tril, g[LANES:, LANES:], 0.0), gram)
        t = _neumann_inverse(n_ab)
        akv = _each(_mm, n_ak, v_st)
        z = _each(lambda t_, a, x: _mm(t_, jnp.concatenate([a, x], axis=1)), t, a_st, akv)
        rz = _each(_mm, n_rb, z)
        rkv = _each(_mm, n_rk, v_st)
        b_st = _each(lambda b, g: split(b * g), bt, g_end)
        k_st = _each(lambda k, g: split(k * g), kt, g_end)
        pm = _each(lambda b, z_: _mm_tn(b, z_[:, :LANES]), b_st, z)
        qm = _each(lambda b, k, z_, v: _mm_tn(jnp.concatenate([b, k], axis=0),
                                              jnp.concatenate([z_[:, LANES:], v], axis=0)), b_st, k_st, z, v_st)
        for i, (pr, c) in enumerate(chains):
            sq = pl.ds(pl.multiple_of(c * LANES, LANES), LANES)
            ra_s[pr, rows[i], :] = (rt[i] + unsplit(rz[i][:, :LANES])).astype(BF16)
            y1_s[pr, rows[i], :] = unsplit(rkv[i] + rz[i][:, LANES:])
            p_s[pr, sq, :] = pm[i].astype(BF16)
            q_s[pr, sq, :] = qm[i]
            gam_s[pr, sq, :] = jnp.broadcast_to(g_end[i], (LANES, LANES)).T
        return carry

    lax.fori_loop(0, ts // CHUNK // unroll, prepare, 0)

    def recur(c, hs):
        rows = pl.ds(pl.multiple_of(c * CHUNK, CHUNK), CHUNK)
        sq = pl.ds(pl.multiple_of(c * LANES, LANES), LANES)
        hb = [h.astype(BF16) for h in hs]
        for pr in range(npairs):
            y1_s[pr, rows, :] = jnp.dot(ra_s[pr, rows, :], hb[pr], preferred_element_type=F32) + y1_s[pr, rows, :]
        return tuple(gam_s[pr, sq, :] * hs[pr] + jnp.dot(p_s[pr, sq, :], hb[pr], preferred_element_type=F32)
                     + q_s[pr, sq, :] for pr in range(npairs))

    final = lax.fori_loop(0, ts // CHUNK, recur, tuple(state_s[pr] for pr in range(npairs)))
    for pr in range(npairs):
        state_s[pr] = final[pr]

    same_head = ((_iota((2 * LANES, LANES), 0) % LANES < HEAD64)
                 == (_iota((2 * LANES, LANES), 1) < HEAD64)).astype(BF16)

    def finish(i, carry):
        rows = pl.ds(pl.multiple_of(i * 2 * CHUNK, 2 * CHUNK), 2 * CHUNK)
        lns = [slice(pr * LANES, (pr + 1) * LANES) for pr in range(npairs)]
        y = [y1_s[pr, rows, :] for pr in range(npairs)]

        def head_sum(x):
            hi = x.astype(BF16)
            lo = (x - hi.astype(F32)).astype(BF16)
            return jnp.dot(jnp.concatenate([hi, lo], axis=1), same_head, preferred_element_type=F32)

        rkk = [head_sum(r_ref[0, rows, ln] * k_ref[0, rows, ln] * rk_ref[:, ln]) for ln in lns]
        yc = _each(lambda y_: y_ - head_sum(y_) * (1.0 / HEAD64), y)
        var = _each(lambda c: head_sum(c * c) * (1.0 / HEAD64), yc)
        for pr, ln in enumerate(lns):
            out = (yc[pr] * lax.rsqrt(var[pr] + RWKV_GN_EPS) * lng_ref[:, ln] + lnb_ref[:, ln]
                   + rkk[pr] * v_ref[0, rows, ln])
            o_ref[0, rows, ln] = out.astype(o_ref.dtype)
        return carry

    lax.fori_loop(0, ts // (2 * CHUNK), finish, 0)


def _rwkv(p_rwkv, mu, w0, w_up, a0, a_up, k_k, k_a, r_k, ln_g, ln_b, tm=256, ts=1024, npairs=4, unroll=2):
    bsz, seq, _ = p_rwkv.shape
    tm, ts = min(tm, seq), min(ts, seq)
    unroll = min(unroll, ts // CHUNK)
    width = npairs * LANES
    nb = GROUP // width
    zeros = jnp.zeros((LORA, GROUP), F32)
    wup = jnp.concatenate([w_up, zeros], axis=0).astype(BF16)
    aup = jnp.concatenate([zeros, a_up], axis=0).astype(BF16)
    row = lambda v: v.reshape(1, -1)
    full = lambda n: pl.BlockSpec((1, n), lambda b, i: (0, 0))
    out_spec = pl.BlockSpec((1, tm, GROUP), lambda b, i: (b, i, 0))
    seq_f32 = jax.ShapeDtypeStruct((bsz, seq, GROUP), F32)
    r, lw, k, v, kk, kka = pl.pallas_call(
        _rwkv_prep_body, grid=(bsz, seq // tm),
        in_specs=[pl.BlockSpec((1, tm, N_RWKV), lambda b, i: (b, i, 0)),
                  pl.BlockSpec((1, 8, N_RWKV), lambda b, i: (b, jnp.maximum(i * (tm // 8) - 1, 0), 0)),
                  full(N_RWKV), full(GROUP),
                  pl.BlockSpec((2 * LORA, GROUP), lambda b, i: (0, 0)), full(GROUP),
                  pl.BlockSpec((2 * LORA, GROUP), lambda b, i: (0, 0)), full(GROUP), full(GROUP)],
        out_specs=[out_spec] * 6, out_shape=[seq_f32] * 6, compiler_params=_params(2),
        name="rwkv_prep")(p_rwkv, p_rwkv, row(mu), row(w0), wup, row(a0), aup, row(k_k), row(k_a))
    n_chunks = ts // CHUNK
    seq_spec = pl.BlockSpec((1, ts, width), lambda b, h, s: (b, s, h))
    par_spec = pl.BlockSpec((1, width), lambda b, h, s: (0, h))
    return pl.pallas_call(
        functools.partial(_rwkv_body, ts=ts, npairs=npairs, unroll=unroll), grid=(bsz, nb, seq // ts),
        in_specs=[seq_spec] * 6 + [par_spec] * 3,
        out_specs=seq_spec, out_shape=jax.ShapeDtypeStruct((bsz, seq, GROUP), BF16),
        scratch_shapes=[pltpu.VMEM((npairs, ts, LANES), BF16), pltpu.VMEM((npairs, ts, LANES), F32),
                        pltpu.VMEM((npairs, n_chunks * LANES, LANES), BF16),
                        pltpu.VMEM((npairs, n_chunks * LANES, LANES), F32),
                        pltpu.VMEM((npairs, n_chunks * LANES, LANES), F32),
                        pltpu.VMEM((npairs, LANES, LANES), F32)],
        compiler_params=_params(3), name="rwkv7")(
            r, lw, k, v, kk, kka, row(r_k), row(ln_g), row(ln_b))


def _split_w_in(w_in):
    b0, b1, b2, b3 = N_RWKV, N_RWKV + N_FOX, N_RWKV + N_FOX + N_GDN, N_RWKV + N_FOX + N_GDN + N_DIFF
    pad = lambda w, n: jnp.pad(w, ((0, 0), (0, 0), (0, n - w.shape[-1])))
    cols = lambda lo, hi: w_in[..., lo:hi].astype(BF16)
    w_a = jnp.concatenate([cols(0, b0), pad(cols(b0, b1), SEG_PAD)], axis=-1)
    w_b = jnp.concatenate([pad(cols(b1, b2), GDN_PAD), cols(b2, b3)], axis=-1)
    return w_a, w_b, cols(b3, w_in.shape[-1])


def kernel(x, norm_g, w_in, w_out, rwkv_mu, rwkv_w0, rwkv_w_up, rwkv_a0, rwkv_a_up, rwkv_k_k, rwkv_k_a, rwkv_r_k, rwkv_ln_g, rwkv_ln_b, fox_q_g, fox_k_g, fox_f_b, gdn_conv, gdn_a_log, gdn_dt_bias, gdn_norm_g, diff_q_g, diff_k_g, diff_lq1, diff_lk1, diff_lq2, diff_lk2, diff_subln_g):
    bsz, seq, d = x.shape
    m = bsz * seq
    w_a, w_b, w_z = _split_w_in(w_in)
    w_out_b = w_out.astype(BF16)
    x2d = x.reshape(m, d)
    for l in range(w_in.shape[0]):
        h = _rmsnorm(x2d, norm_g[l])
        p_a = _matmul(h, w_a[l], IN_PROJ_TN, F32).reshape(bsz, seq, -1)
        p_b = _matmul(h, w_b[l], IN_PROJ_TN, F32).reshape(bsz, seq, -1)
        sz = _matmul(h, w_z[l], 1024, BF16, silu=True, name="in_proj_gate")
        y_rwkv = _rwkv(p_a, rwkv_mu[l], rwkv_w0[l], rwkv_w_up[l], rwkv_a0[l], rwkv_a_up[l], rwkv_k_k[l],
                       rwkv_k_a[l], rwkv_r_k[l].reshape(-1), rwkv_ln_g[l], rwkv_ln_b[l])
        y_fox = _fox(p_a, fox_f_b[l], fox_q_g[l], fox_k_g[l], col0=N_RWKV)
        y_gdn = _gdn(p_b, gdn_conv[l], gdn_a_log[l], gdn_dt_bias[l], gdn_norm_g[l])
        y_diff = _diff(p_b, l, diff_q_g[l], diff_k_g[l], diff_lq1[l], diff_lk1[l], diff_lq2[l], diff_lk2[l],
                       diff_subln_g[l], col0=GDN_PAD)
        ys = [y.reshape(m, GROUP) for y in (y_rwkv, y_fox, y_gdn, y_diff)]
        x2d = _outproj(x2d, ys, sz, w_out_b[l])
    return x2d.reshape(bsz, seq, d)
```

```python
import functools
import math

import jax
import jax.numpy as jnp
from jax import lax
from jax.experimental import pallas as pl
from jax.experimental.pallas import tpu as pltpu

F32 = jnp.float32
BF16 = jnp.bfloat16
HI = lax.Precision.HIGHEST

D_MODEL = 2048
DEPTH = 4
GROUP = 1024
D_MIX = 4 * GROUP
LANES = 128
HEAD64 = 64
GDN_HEAD = 128
CHUNK = 64
LORA = 64
RWKV_GN_EPS = 64e-5
NORM_EPS = 1e-6
GDN_CONV = 4
N_RWKV = 3 * GROUP + 2 * LORA
N_FOX = 3 * GROUP + 16
N_GDN = 3 * GROUP + 16
N_DIFF = 3 * GROUP
SEG_PAD = 3200
GDN_PAD = 3328
IN_PROJ_TN = 1280
VMEM_LIMIT = 56 * 1024 * 1024

NT = (((1,), (1,)), ((), ()))
TN = (((0,), (0,)), ((), ()))


def _params(n_axes):
    return pltpu.CompilerParams(dimension_semantics=("arbitrary",) * n_axes,
                                vmem_limit_bytes=VMEM_LIMIT)


def _softplus(x):
    return jnp.maximum(x, 0.0) + jnp.log1p(jnp.exp(-jnp.abs(x)))


def _iota(shape, axis):
    return lax.broadcasted_iota(jnp.int32, shape, axis)


def _each(fn, *lists):
    return [fn(*args) for args in zip(*lists)]


def _half_sum(x, low):
    s0 = jnp.sum(jnp.where(low, x, 0.0), axis=1, keepdims=True)
    s1 = jnp.sum(jnp.where(low, 0.0, x), axis=1, keepdims=True)
    return jnp.where(low, s0, s1)


def _mm(a, b):
    return jnp.dot(a.astype(BF16), b.astype(BF16), preferred_element_type=F32)


def _mm_nt(a, b):
    return lax.dot_general(a.astype(BF16), b.astype(BF16), NT, preferred_element_type=F32)


def _mm_tn(a, b):
    return lax.dot_general(a.astype(BF16), b.astype(BF16), TN, preferred_element_type=F32)


def _neumann_inverse(ns):
    eye = (_iota((LANES, LANES), 0) == _iota((LANES, LANES), 1)).astype(F32)

    ts = [eye + n for n in ns]
    ps = _each(_mm, ns, ns)
    for _ in range(4):
        tp = _each(lambda t, p: _mm(jnp.concatenate([t, p], axis=0), p), ts, ps)
        ts = _each(lambda t, x: t + x[:LANES], ts, tp)
        ps = _each(lambda x: x[LANES:], tp)
    return _each(lambda t, p: t + _mm(t, p), ts, ps)


def _block_masks():
    ri = _iota((LANES, LANES), 0)
    ci = _iota((LANES, LANES), 1)
    same = (ri < CHUNK) == (ci < CHUNK)
    return same & (ri >= ci), same & (ri > ci), same & (ri <= ci), same


def _rmsnorm_body(x_ref, g_ref, o_ref):
    x = x_ref[...]
    ms = jnp.mean(x * x, axis=-1, keepdims=True)
    o_ref[...] = (x * lax.rsqrt(ms + NORM_EPS) * g_ref[...]).astype(o_ref.dtype)


def _rmsnorm(x2d, g, tm=512):
    m, d = x2d.shape
    return pl.pallas_call(
        _rmsnorm_body, grid=(m // tm,),
        in_specs=[pl.BlockSpec((tm, d), lambda i: (i, 0)), pl.BlockSpec((1, d), lambda i: (0, 0))],
        out_specs=pl.BlockSpec((tm, d), lambda i: (i, 0)),
        out_shape=jax.ShapeDtypeStruct((m, d), BF16), compiler_params=_params(1),
        name="rmsnorm")(x2d, g.reshape(1, d))


def _matmul_body(h_ref, w_ref, o_ref, *, silu):
    acc = jnp.dot(h_ref[...], w_ref[...], preferred_element_type=F32)
    if silu:
        acc = acc * jax.nn.sigmoid(acc)
    o_ref[...] = acc.astype(o_ref.dtype)


def _matmul(h, w, tn, out_dtype, silu=False, tm=1024, name="in_proj"):
    m, k = h.shape
    n = w.shape[1]
    tm = min(tm, m)
    return pl.pallas_call(
        functools.partial(_matmul_body, silu=silu), grid=(m // tm, n // tn),
        in_specs=[pl.BlockSpec((tm, k), lambda i, j: (i, 0)), pl.BlockSpec((k, tn), lambda i, j: (0, j))],
        out_specs=pl.BlockSpec((tm, tn), lambda i, j: (i, j)),
        out_shape=jax.ShapeDtypeStruct((m, n), out_dtype), compiler_params=_params(2),
        name=name)(h, w)


def _outproj_body(x_ref, yr_ref, yf_ref, yg_ref, yd_ref, sz_ref, w_ref, o_ref):
    acc = x_ref[...]
    for g, y_ref in enumerate((yr_ref, yf_ref, yg_ref, yd_ref)):
        gate = (y_ref[...].astype(F32) * sz_ref[:, g * GROUP:(g + 1) * GROUP].astype(F32)).astype(BF16)
        acc = acc + jnp.dot(gate, w_ref[g * GROUP:(g + 1) * GROUP, :], preferred_element_type=F32)
    o_ref[...] = acc


def _outproj(x2d, ys, sz, w_out, tm=512, tn=1024):
    m, d = x2d.shape
    tm = min(tm, m)
    yspec = pl.BlockSpec((tm, GROUP), lambda j, i: (i, 0))
    return pl.pallas_call(
        _outproj_body, grid=(d // tn, m // tm),
        in_specs=[pl.BlockSpec((tm, tn), lambda j, i: (i, j)), yspec, yspec, yspec, yspec,
                  pl.BlockSpec((tm, D_MIX), lambda j, i: (i, 0)),
                  pl.BlockSpec((D_MIX, tn), lambda j, i: (0, j))],
        out_specs=pl.BlockSpec((tm, tn), lambda j, i: (i, j)),
        out_shape=jax.ShapeDtypeStruct((m, d), F32), compiler_params=_params(2),
        name="out_proj")(x2d, *ys, sz, w_out)


def _aug_lanes(h):
    first = (1 - h) * HEAD64
    lane = _iota((1, LANES), 1)
    return lane == first, lane == first + 1


ONES_ROWS = 16
LOG2E = 1.4426950408889634


def _attend(qts, kn_ref, vt_ref, sa_ref, sb_ref, acc_ref, sigma_fns, qi, tq, tk):
    q_pos = qi * tq + _iota((1, tq), 1)
    heads = range(len(qts))

    def produce(j, s_ref, diagonal=False):
        k0 = pl.multiple_of(j * tk, tk)
        s = [jnp.dot(kn_ref[h, pl.ds(k0, tk), :], qts[h], preferred_element_type=F32) for h in heads]
        if diagonal:
            visible = (k0 + _iota((tk, 1), 0)) <= q_pos
            s = [jnp.where(visible, x, -jnp.inf) for x in s]
        for h in heads:
            s_ref[h] = s[h]
        return [jnp.max(x, axis=0, keepdims=True) for x in s]

    def consume(j, s_ref, s_max, m):
        sigma = [sigma_fns[h](j) * LOG2E for h in heads]
        m_new = [jnp.maximum(m[h], s_max[h] + sigma[h]) for h in heads]
        alpha = [jnp.exp2(m[h] - m_new[h]) for h in heads]
        p = [jnp.exp2(s_ref[h] - (m_new[h] - sigma[h])).astype(BF16) for h in heads]
        pv = [jnp.dot(vt_ref[h % vt_ref.shape[0], j], p[h], preferred_element_type=F32) for h in heads]
        for h in heads:
            acc_ref[h] = alpha[h] * acc_ref[h] + pv[h]
        return m_new

    n_full = (qi * tq) // tk
    acc_ref[...] = jnp.zeros_like(acc_ref)
    max_a = produce(n_full, sa_ref, diagonal=True)

    def pair(t, carry):
        m, max_a, in_a = carry
        max_b = produce(2 * t, sb_ref)
        m = consume(in_a, sa_ref, max_a, m)
        max_a = produce(2 * t + 1, sa_ref)
        m = consume(2 * t, sb_ref, max_b, m)
        return m, max_a, 2 * t + 1

    m, max_a, in_a = lax.fori_loop(0, n_full // 2, pair,
                                   ([jnp.full((1, tq), -jnp.inf, F32)] * 2, max_a, n_full))

    @pl.when(n_full % 2 == 1)
    def _():
        max_b = produce(n_full - 1, sb_ref)
        consume(n_full - 1, sb_ref, max_b, consume(in_a, sa_ref, max_a, m))

    @pl.when(n_full % 2 == 0)
    def _():
        consume(in_a, sa_ref, max_a, m)


def _half_rmsnorm(x, gain, low):
    ms = _half_sum(x * x, low) * (1.0 / HEAD64)
    return x * lax.rsqrt(ms + NORM_EPS) * gain


def _augment_queries(qn):
    qt = qn.T
    row = _iota((LANES, 1), 0)
    out = []
    for h in range(2):
        first = (1 - h) * HEAD64
        own = row < HEAD64 if h == 0 else row >= HEAD64
        out.append(jnp.where(own, qt, jnp.where((row == first) | (row == first + 1), 1.0, 0.0)).astype(BF16))
    return out


def _normalised_output(acc):
    dv = acc.shape[0] - ONES_ROWS
    return acc[:dv] / acc[dv:dv + 1]


def _stage_keys(k_ref, v_ref, kg_ref, kn_s, vt_s, low, seq, tk, key_bias):
    def body(i, c):
        r0 = pl.multiple_of(i * tk, tk)
        rows = pl.ds(r0, tk)
        kn = _half_rmsnorm(k_ref[0, rows, :], kg_ref[...], low)
        for h in range(2):
            d = key_bias(h, r0) * LOG2E
            d_hi = d.astype(BF16).astype(F32)
            a0, a1 = _aug_lanes(h)
            aug = jnp.where(a0, d_hi, jnp.where(a1, d - d_hi, 0.0))
            kn_s[h, rows, :] = jnp.where(low if h == 0 else ~low, kn, aug).astype(BF16)
        vt = v_ref[0, rows, :].T.astype(BF16)
        dv = vt_s.shape[2] - ONES_ROWS
        for g in range(vt_s.shape[0]):
            vt_s[g, i, :dv, :] = vt[g * dv:(g + 1) * dv]
            vt_s[g, i, dv:, :] = jnp.ones((ONES_ROWS, tk), BF16)
        return c

    lax.fori_loop(0, seq // tk, body, 0)


def _fox_body(q_ref, k_ref, v_ref, ck_ref, cq_ref, cc_ref, qg_ref, kg_ref, o_ref, kn_s, vb_s, sa_s, sb_s, acc_s,
              *, seq, tq, tk):
    hp = pl.program_id(1)
    qi = pl.program_id(2)
    low = _iota((1, LANES), 1) < HEAD64

    def key_bias(h, r0):
        c_blk = jnp.sum(jnp.where(_iota((1, 16), 1) == 2 * hp + h, cc_ref[0, pl.ds(r0, tk), :], 0.0),
                        axis=1, keepdims=True)
        return c_blk[0:1, :] - c_blk

    @pl.when(qi == 0)
    def _():
        _stage_keys(k_ref, v_ref, kg_ref, kn_s, vb_s, low, seq, tk, key_bias)

    qn = _half_rmsnorm(q_ref[0], qg_ref[...], low) * (HEAD64 ** -0.5 * LOG2E)
    sigma_fns = []
    for h in range(2):
        row = 2 * hp + h
        c_q0 = cq_ref[0, row, pl.ds(qi, 1), :][:, 0:1]
        sigma_fns.append(lambda j, row=row, c_q0=c_q0: c_q0 - ck_ref[0, row, pl.ds(j, 1), :][:, 0:1])
    _attend(_augment_queries(qn), kn_s, vb_s, sa_s, sb_s, acc_s, sigma_fns, qi, tq, tk)
    o_ref[0] = jnp.concatenate([_normalised_output(acc_s[0]), _normalised_output(acc_s[1])],
                               axis=0).T.astype(o_ref.dtype)


def _fox_cum_body(f_ref, fb_ref, c_ref, *, seq):
    x = f_ref[0] + fb_ref[...]
    logf = jnp.minimum(x, 0.0) - jnp.log1p(jnp.exp(-jnp.abs(x)))
    upper = (_iota((LANES, LANES), 0) <= _iota((LANES, LANES), 1)).astype(F32)
    carry = jnp.zeros((16, 1), F32)
    for blk in range(seq // LANES):
        sl = slice(blk * LANES, (blk + 1) * LANES)
        cb = jnp.dot(logf[:, sl], upper, precision=HI, preferred_element_type=F32) + carry
        c_ref[0, :, sl] = cb
        carry = cb[:, LANES - 1:LANES]


def _attention_scratch(seq, tq, tk, dv):
    return [pltpu.VMEM((2, seq, LANES), BF16), pltpu.VMEM((LANES // dv, seq // tk, dv + ONES_ROWS, tk), BF16),
            pltpu.VMEM((2, tk, tq), F32), pltpu.VMEM((2, tk, tq), F32),
            pltpu.VMEM((2, dv + ONES_ROWS, tq), F32)]


def _fox(p_fox, f_b, q_g, k_g, col0=0, tq=512, tk=512):
    bsz, seq, _ = p_fox.shape
    tq, tk = min(tq, seq), min(tk, seq)
    cb = col0 // LANES
    f_t = jnp.transpose(p_fox[:, :, col0 + 3 * GROUP:col0 + 3 * GROUP + 16], (0, 2, 1))
    c = pl.pallas_call(
        functools.partial(_fox_cum_body, seq=seq), grid=(bsz,),
        in_specs=[pl.BlockSpec((1, 16, seq), lambda b: (b, 0, 0)), pl.BlockSpec((16, 1), lambda b: (0, 0))],
        out_specs=pl.BlockSpec((1, 16, seq), lambda b: (b, 0, 0)),
        out_shape=jax.ShapeDtypeStruct((bsz, 16, seq), F32), compiler_params=_params(1),
        name="fox_cumsum")(f_t, f_b.reshape(16, 1))
    nb = GROUP // LANES
    gain = lambda g: jnp.tile(g, 2).reshape(1, LANES)
    return pl.pallas_call(
        functools.partial(_fox_body, seq=seq, tq=tq, tk=tk), grid=(bsz, nb, seq // tq),
        in_specs=[pl.BlockSpec((1, tq, LANES), lambda b, h, i: (b, i, cb + h)),
                  pl.BlockSpec((1, seq, LANES), lambda b, h, i: (b, 0, cb + nb + h)),
                  pl.BlockSpec((1, seq, LANES), lambda b, h, i: (b, 0, cb + 2 * nb + h)),
                  pl.BlockSpec((1, 16, seq // tk, tk), lambda b, h, i: (b, 0, 0, 0)),
                  pl.BlockSpec((1, 16, seq // tq, tq), lambda b, h, i: (b, 0, 0, 0)),
                  pl.BlockSpec((1, seq, 16), lambda b, h, i: (b, 0, 0)),
                  pl.BlockSpec((1, LANES), lambda b, h, i: (0, 0)),
                  pl.BlockSpec((1, LANES), lambda b, h, i: (0, 0))],
        out_specs=pl.BlockSpec((1, tq, LANES), lambda b, h, i: (b, i, h)),
        out_shape=jax.ShapeDtypeStruct((bsz, seq, GROUP), BF16),
        scratch_shapes=_attention_scratch(seq, tq, tk, HEAD64),
        compiler_params=_params(3), name="fox_attention")(
            p_fox, p_fox, p_fox, c.reshape(bsz, 16, seq // tk, tk), c.reshape(bsz, 16, seq // tq, tq),
            jnp.transpose(c, (0, 2, 1)), gain(q_g), gain(k_g))


def _diff_body(q_ref, k_ref, v_ref, qg_ref, kg_ref, lam_ref, sg_ref, o_ref, kn_s, vb_s, sa_s, sb_s, acc_s,
               *, seq, tq, tk, lam_init):
    head = pl.program_id(1)
    qi = pl.program_id(2)
    low = _iota((1, LANES), 1) < HEAD64
    slope = jnp.exp2(-(head + 1).astype(F32) * jnp.ones((1, 1), F32))

    @pl.when(qi == 0)
    def _():
        in_block = _iota((tk, 1), 0).astype(F32)
        _stage_keys(k_ref, v_ref, kg_ref, kn_s, vb_s, low, seq, tk, lambda h, r0: slope * in_block)

    lq1, lk1, lq2, lk2 = (lam_ref[i:i + 1, :] for i in range(4))
    lam = (jnp.exp(jnp.sum(lq1 * lk1, axis=1, keepdims=True))
           - jnp.exp(jnp.sum(lq2 * lk2, axis=1, keepdims=True)) + lam_init)

    def sigma(j):
        return slope * (j * tk - qi * tq).astype(F32)

    qn = _half_rmsnorm(q_ref[0], qg_ref[...], low) * (HEAD64 ** -0.5 * LOG2E)
    _attend(_augment_queries(qn), kn_s, vb_s, sa_s, sb_s, acc_s, [sigma, sigma], qi, tq, tk)
    o = (_normalised_output(acc_s[0]) - lam * _normalised_output(acc_s[1])).T
    ms = jnp.mean(o * o, axis=1, keepdims=True)
    o_ref[0] = (o * lax.rsqrt(ms + 1e-5) * sg_ref[...] * (1.0 - lam_init)).astype(o_ref.dtype)


def _diff(p_diff, layer, q_g, k_g, lq1, lk1, lq2, lk2, subln_g, col0=0, tq=512, tk=512):
    bsz, seq, _ = p_diff.shape
    tq, tk = min(tq, seq), min(tk, seq)
    nb = GROUP // LANES
    cb = col0 // LANES
    lam_init = 0.8 - 0.6 * math.exp(-0.3 * layer)
    gain = lambda g: jnp.tile(g, 2).reshape(1, LANES)
    lam_rows = jnp.stack([lq1, lk1, lq2, lk2])
    return pl.pallas_call(
        functools.partial(_diff_body, seq=seq, tq=tq, tk=tk, lam_init=lam_init), grid=(bsz, nb, seq // tq),
        in_specs=[pl.BlockSpec((1, tq, LANES), lambda b, h, i: (b, i, cb + h)),
                  pl.BlockSpec((1, seq, LANES), lambda b, h, i: (b, 0, cb + nb + h)),
                  pl.BlockSpec((1, seq, LANES), lambda b, h, i: (b, 0, cb + 2 * nb + h)),
                  pl.BlockSpec((1, LANES), lambda b, h, i: (0, 0)),
                  pl.BlockSpec((1, LANES), lambda b, h, i: (0, 0)),
                  pl.BlockSpec((4, HEAD64), lambda b, h, i: (0, 0)),
                  pl.BlockSpec((1, LANES), lambda b, h, i: (0, 0))],
        out_specs=pl.BlockSpec((1, tq, LANES), lambda b, h, i: (b, i, h)),
        out_shape=jax.ShapeDtypeStruct((bsz, seq, GROUP), BF16),
        scratch_shapes=_attention_scratch(seq, tq, tk, 2 * HEAD64),
        compiler_params=_params(3), name="diff_attention")(
            p_diff, p_diff, p_diff, gain(q_g), gain(k_g), lam_rows, subln_g.reshape(1, LANES))


def _gdn_body(q_ref, k_ref, v_ref, qh_ref, kh_ref, vh_ref, gate_ref, cwq_ref, cwk_ref, cwv_ref, alog_ref, dt_ref,
              ng_ref, o_ref, qw_s, au_s, p_s, q_s, eg_s, state_s, *, ts, nheads, unroll):
    pair = 2 * CHUNK
    first_tile = pl.program_id(2) == 0
    lane = _iota((1, LANES), 1)
    pick = lambda x, idx: jnp.sum(jnp.where(lane == idx, x, 0.0), axis=1, keepdims=True)
    m_tril, m_strict, m_upper, m_same = _block_masks()
    eye = _iota((pair, pair), 0) == _iota((pair, pair), 1)
    head_ids = [pl.program_id(1) * nheads + hd for hd in range(nheads)]
    a_scale_row = -jnp.exp(alog_ref[...])

    @pl.when(first_tile)
    def _():
        state_s[...] = jnp.zeros_like(state_s)

    def prepare(it, carry):
        chains = [(hd, it * unroll + u) for u in range(unroll) for hd in range(nheads)]
        lanes = [slice(hd * LANES, (hd + 1) * LANES) for hd, _ in chains]
        r0s = [pl.multiple_of(cp * pair, pair) for _, cp in chains]

        def conv_silu(ref, halo_ref, cw_ref):
            def one(ln, r0, chain):
                cur = ref[0, pl.ds(r0, pair), ln]
                inside = ref[0, pl.ds(pl.multiple_of(jnp.maximum(r0 - 8, 0), 8), 8), ln]
                before = jnp.where(first_tile, 0.0, halo_ref[0, :, ln])
                ext = jnp.concatenate([jnp.where(chain[1] == 0, before, inside), cur], axis=0)
                acc = cur * cw_ref[GDN_CONV - 1:GDN_CONV, ln]
                for j in range(1, GDN_CONV):
                    acc = acc + pltpu.roll(ext, j, 0)[8:, :] * cw_ref[GDN_CONV - 1 - j:GDN_CONV - j, ln]
                return acc * jax.nn.sigmoid(acc)
            return _each(one, lanes, r0s, chains)

        q = conv_silu(q_ref, qh_ref, cwq_ref)
        k = conv_silu(k_ref, kh_ref, cwk_ref)
        v = conv_silu(v_ref, vh_ref, cwv_ref)
        q = _each(lambda x: x * lax.rsqrt(jnp.sum(x * x, axis=1, keepdims=True) + 1e-6) * (GDN_HEAD ** -0.5), q)
        k = _each(lambda x: x * lax.rsqrt(jnp.sum(x * x, axis=1, keepdims=True) + 1e-6), k)
        decay_log, betas = {}, {}
        for u in range(unroll):
            blk = gate_ref[0, pl.ds(r0s[u * nheads], pair), :]
            decay_log[u] = a_scale_row * _softplus(blk + dt_ref[...])
            betas[u] = jax.nn.sigmoid(blk)
        g_col = [pick(decay_log[i // nheads], head_ids[hd]) for i, (hd, _) in enumerate(chains)]
        beta = [pick(betas[i // nheads], head_ids[hd] + 8) for i, (hd, _) in enumerate(chains)]
        g_row = _each(lambda g: jnp.sum(jnp.where(eye, g, 0.0), axis=0, keepdims=True), g_col)
        gc_col = _each(lambda g: jnp.sum(jnp.where(m_tril, g, 0.0), axis=1, keepdims=True), g_row)
        gc_row = _each(lambda g: jnp.sum(jnp.where(m_upper, g, 0.0), axis=0, keepdims=True), g_col)
        g_last = _each(lambda g: jnp.sum(jnp.where(m_same, g, 0.0), axis=1, keepdims=True), g_row)
        decay = _each(lambda c, r: jnp.exp(jnp.where(m_tril, c - r, -jnp.inf)), gc_col, gc_row)
        exp_gc = _each(jnp.exp, gc_col)
        kb = _each(lambda x, b: x * b, k, beta)
        kk = _each(_mm_nt, kb, k)
        t = _neumann_inverse(_each(lambda x, d: -jnp.where(m_strict, x * d, 0.0), kk, decay))
        uw = _each(lambda t_, v_, b, kb_, e: _mm(t_, jnp.concatenate([v_ * b, kb_ * e], axis=1)),
                   t, v, beta, kb, exp_gc)
        ai = _each(lambda q_, k_, d: _mm_nt(q_, k_) * d, q, k, decay)
        aiuw = _each(_mm, ai, uw)
        kd = _each(lambda k_, gl, gc: k_ * jnp.exp(gl - gc), k, g_last, gc_col)
        kuw = [[_mm_tn(kd_[half * CHUNK:(half + 1) * CHUNK], uw_[half * CHUNK:(half + 1) * CHUNK])
                for kd_, uw_ in zip(kd, uw)] for half in range(2)]
        for i, (hd, cp) in enumerate(chains):
            rows = pl.ds(r0s[i], pair)
            qw_s[hd, rows, :] = (q[i] * exp_gc[i] - aiuw[i][:, LANES:]).astype(BF16)
            au_s[hd, rows, :] = aiuw[i][:, :LANES]
            e_last = jnp.exp(g_last[i])
            for half in range(2):
                c = 2 * cp + half
                sq = pl.ds(pl.multiple_of(c * LANES, LANES), LANES)
                p_s[hd, sq, :] = (-kuw[half][i][:, LANES:]).astype(BF16)
                q_s[hd, sq, :] = kuw[half][i][:, :LANES]
                eg_s[hd, pl.ds(c, 1), :] = jnp.broadcast_to(e_last[half * CHUNK:half * CHUNK + 1], (1, LANES))
        return carry

    lax.fori_loop(0, ts // pair // unroll, prepare, 0)

    def recur(c, states):
        rows = pl.ds(pl.multiple_of(c * CHUNK, CHUNK), CHUNK)
        sq = pl.ds(pl.multiple_of(c * LANES, LANES), LANES)
        sb = [s.astype(BF16) for s in states]
        for hd in range(nheads):
            au_s[hd, rows, :] = jnp.dot(qw_s[hd, rows, :], sb[hd], preferred_element_type=F32) + au_s[hd, rows, :]
        return tuple(states[hd] * eg_s[hd, pl.ds(c, 1), :]
                     + jnp.dot(p_s[hd, sq, :], sb[hd], preferred_element_type=F32) + q_s[hd, sq, :]
                     for hd in range(nheads))

    final = lax.fori_loop(0, ts // CHUNK, recur, tuple(state_s[hd] for hd in range(nheads)))
    for hd in range(nheads):
        state_s[hd] = final[hd]

    def finish(i, carry):
        rows = pl.ds(pl.multiple_of(i * 4 * CHUNK, 4 * CHUNK), 4 * CHUNK)
        for hd in range(nheads):
            o = au_s[hd, rows, :]
            ms = jnp.mean(o * o, axis=1, keepdims=True)
            o_ref[0, rows, hd * LANES:(hd + 1) * LANES] = (o * lax.rsqrt(ms + NORM_EPS) * ng_ref[...]).astype(o_ref.dtype)
        return carry

    lax.fori_loop(0, ts // (4 * CHUNK), finish, 0)


def _gdn(p_gdn, conv_w, a_log, dt_bias, norm_g, ts=1024, nheads=4, unroll=1):
    bsz, seq, _ = p_gdn.shape
    ts = min(ts, seq)
    nb = GROUP // LANES // nheads
    width = nheads * LANES
    n_chunks = ts // CHUNK
    pad = lambda v: jnp.pad(v, (0, LANES - v.shape[0])).reshape(1, LANES)
    seq_spec = lambda off: pl.BlockSpec((1, ts, width), lambda b, h, s: (b, s, off + h))
    halo_spec = lambda off: pl.BlockSpec((1, 8, width), lambda b, h, s: (b, jnp.maximum(s * (ts // 8) - 1, 0), off + h))
    cw_spec = lambda off: pl.BlockSpec((GDN_CONV, width), lambda b, h, s: (0, off + h))
    row_spec = pl.BlockSpec((1, LANES), lambda b, h, s: (0, 0))
    return pl.pallas_call(
        functools.partial(_gdn_body, ts=ts, nheads=nheads, unroll=unroll), grid=(bsz, nb, seq // ts),
        in_specs=[seq_spec(0), seq_spec(nb), seq_spec(2 * nb), halo_spec(0), halo_spec(nb), halo_spec(2 * nb),
                  pl.BlockSpec((1, ts, LANES), lambda b, h, s: (b, s, 3 * GROUP // LANES)),
                  cw_spec(0), cw_spec(nb), cw_spec(2 * nb), row_spec, row_spec, row_spec],
        out_specs=pl.BlockSpec((1, ts, width), lambda b, h, s: (b, s, h)),
        out_shape=jax.ShapeDtypeStruct((bsz, seq, GROUP), BF16),
        scratch_shapes=[pltpu.VMEM((nheads, ts, LANES), BF16), pltpu.VMEM((nheads, ts, LANES), F32),
                        pltpu.VMEM((nheads, n_chunks * LANES, LANES), BF16),
                        pltpu.VMEM((nheads, n_chunks * LANES, LANES), F32),
                        pltpu.VMEM((nheads, max(n_chunks, 8), LANES), F32),
                        pltpu.VMEM((nheads, GDN_HEAD, GDN_HEAD), F32)],
        compiler_params=_params(3), name="gated_deltanet")(
            p_gdn, p_gdn, p_gdn, p_gdn, p_gdn, p_gdn, p_gdn, conv_w, conv_w, conv_w, pad(a_log), pad(dt_bias),
            norm_g.reshape(1, LANES))


def _rwkv_prep_body(p_ref, halo_ref, mu_ref, w0_ref, wup_ref, a0_ref, aup_ref, kk_ref, ka_ref,
                    r_ref, lw_ref, k_ref, v_ref, kkn_ref, kka_ref):
    p = p_ref[0]
    first = pl.program_id(1) == 0
    last_prev = jnp.where(first, 0.0, halo_ref[0, 7:8, :])
    prev = jnp.where(_iota((p.shape[0], 1), 0) == 0, last_prev, pltpu.roll(p, 1, 0))
    xs = p + mu_ref[...] * (prev - p)
    r, k, v = xs[:, :GROUP], xs[:, GROUP:2 * GROUP], xs[:, 2 * GROUP:3 * GROUP]
    lora = xs[:, 3 * GROUP:]
    w = w0_ref[...] + jnp.dot(jnp.tanh(lora).astype(BF16), wup_ref[...], preferred_element_type=F32)
    a = jax.nn.sigmoid(a0_ref[...] + jnp.dot(lora.astype(BF16), aup_ref[...], preferred_element_type=F32))
    low = _iota((1, LANES), 1) < HEAD64
    r_ref[0] = r.astype(r_ref.dtype)
    v_ref[0] = v.astype(v_ref.dtype)
    lw_ref[0] = -jnp.exp(-_softplus(-w) - 0.5)
    k_ref[0] = (k * (1.0 + (a - 1.0) * ka_ref[...])).astype(k_ref.dtype)
    for blk in range(GROUP // LANES):
        sl = slice(blk * LANES, (blk + 1) * LANES)
        kk0 = k[:, sl] * kk_ref[:, sl]
        kkn = kk0 * lax.rsqrt(_half_sum(kk0 * kk0, low) + 1e-6)
        kkn_ref[0, :, sl] = kkn.astype(kkn_ref.dtype)
        kka_ref[0, :, sl] = (kkn * a[:, sl]).astype(kka_ref.dtype)


def _rwkv_body(r_ref, lw_ref, k_ref, v_ref, kk_ref, kka_ref, rk_ref, lng_ref, lnb_ref, o_ref,
               ra_s, y1_s, p_s, q_s, gam_s, state_s, *, ts, npairs, unroll):
    low = _iota((1, LANES), 1) < HEAD64
    m_tril, m_strict, _, _ = _block_masks()
    tril_b = (_iota((CHUNK, CHUNK), 0) >= _iota((CHUNK, CHUNK), 1)).astype(BF16)
    split = lambda x: jnp.concatenate([jnp.where(low, x, 0.0), jnp.where(low, 0.0, x)], axis=0)
    unsplit = lambda x: x[:CHUNK] + x[CHUNK:]
    mul = lambda a, b: a * b

    @pl.when(pl.program_id(2) == 0)
    def _():
        state_s[...] = jnp.zeros_like(state_s)

    def prepare(it, carry):
        chains = [(pr, it * unroll + u) for u in range(unroll) for pr in range(npairs)]
        rows = [pl.ds(pl.multiple_of(c * CHUNK, CHUNK), CHUNK) for _, c in chains]
        lanes = [slice(pr * LANES, (pr + 1) * LANES) for pr, _ in chains]
        ld = lambda ref: [ref[0, rw, ln].astype(F32) for rw, ln in zip(rows, lanes)]
        lw = ld(lw_ref)
        lw_hi = _each(lambda x: x.astype(BF16), lw)
        lw_lo = _each(lambda x, hi: (x - hi.astype(F32)).astype(BF16), lw, lw_hi)
        cum = _each(lambda hi, lo: (jnp.dot(tril_b, hi, preferred_element_type=F32)
                                    + jnp.dot(tril_b, lo, preferred_element_type=F32)), lw_hi, lw_lo)
        gam = _each(jnp.exp, cum)
        inv = _each(lambda c: jnp.exp(-c), cum)
        g_end = _each(lambda g: g[CHUNK - 1:CHUNK, :], gam)
        at = _each(lambda kk, c, w: -kk * jnp.exp(c - w), ld(kk_ref), cum, lw)
        rt = _each(mul, ld(r_ref), gam)
        bt = _each(mul, ld(kka_ref), inv)
        kt = _each(mul, ld(k_ref), inv)
        a_st = _each(split, at)
        v_st = _each(split, ld(v_ref))
        gram = _each(lambda a, r, b, k: _mm_nt(jnp.concatenate([a, split(r)], axis=0),
                                               jnp.concatenate([b, b, k, k], axis=0)), a_st, rt, bt, kt)
        n_ab = _each(lambda g: jnp.where(m_strict, g[:LANES, :LANES], 0.0), gram)
        n_ak = _each(lambda g: jnp.where(m_strict, g[:LANES, LANES:], 0.0), gram)
        n_rb = _each(lambda g: jnp.where(m_tril, g[LANES:, :LANES], 0.0), gram)
        n_rk = _each(lambda g: jnp.where(m_tril, g[LANES:, LANES:], 0.0), gram)
        t = _neumann_inverse(n_ab)
        akv = _each(_mm, n_ak, v_st)
        z = _each(lambda t_, a, x: _mm(t_, jnp.concatenate([a, x], axis=1)), t, a_st, akv)
        rz = _each(_mm, n_rb, z)
        rkv = _each(_mm, n_rk, v_st)
        b_st = _each(lambda b, g: split(b * g), bt, g_end)
        k_st = _each(lambda k, g: split(k * g), kt, g_end)
        pm = _each(lambda b, z_: _mm_tn(b, z_[:, :LANES]), b_st, z)
        qm = _each(lambda b, k, z_, v: _mm_tn(jnp.concatenate([b, k], axis=0),
                                              jnp.concatenate([z_[:, LANES:], v], axis=0)), b_st, k_st, z, v_st)
        for i, (pr, c) in enumerate(chains):
            sq = pl.ds(pl.multiple_of(c * LANES, LANES), LANES)
            ra_s[pr, rows[i], :] = (rt[i] + unsplit(rz[i][:, :LANES])).astype(BF16)
            y1_s[pr, rows[i], :] = unsplit(rkv[i] + rz[i][:, LANES:])
            p_s[pr, sq, :] = pm[i].astype(BF16)
            q_s[pr, sq, :] = qm[i]
            gam_s[pr, sq, :] = jnp.broadcast_to(g_end[i], (LANES, LANES)).T
        return carry

    lax.fori_loop(0, ts // CHUNK // unroll, prepare, 0)

    def recur(c, hs):
        rows = pl.ds(pl.multiple_of(c * CHUNK, CHUNK), CHUNK)
        sq = pl.ds(pl.multiple_of(c * LANES, LANES), LANES)
        hb = [h.astype(BF16) for h in hs]
        for pr in range(npairs):
            y1_s[pr, rows, :] = jnp.dot(ra_s[pr, rows, :], hb[pr], preferred_element_type=F32) + y1_s[pr, rows, :]
        return tuple(gam_s[pr, sq, :] * hs[pr] + jnp.dot(p_s[pr, sq, :], hb[pr], preferred_element_type=F32)
                     + q_s[pr, sq, :] for pr in range(npairs))

    final = lax.fori_loop(0, ts // CHUNK, recur, tuple(state_s[pr] for pr in range(npairs)))
    for pr in range(npairs):
        state_s[pr] = final[pr]

    same_head = ((_iota((2 * LANES, LANES), 0) % LANES < HEAD64)
                 == (_iota((2 * LANES, LANES), 1) < HEAD64)).astype(BF16)

    def finish(i, carry):
        rows = pl.ds(pl.multiple_of(i * 2 * CHUNK, 2 * CHUNK), 2 * CHUNK)
        lns = [slice(pr * LANES, (pr + 1) * LANES) for pr in range(npairs)]
        y = [y1_s[pr, rows, :] for pr in range(npairs)]

        def head_sum(x):
            hi = x.astype(BF16)
            lo = (x - hi.astype(F32)).astype(BF16)
            return jnp.dot(jnp.concatenate([hi, lo], axis=1), same_head, preferred_element_type=F32)

        rkk = [head_sum(r_ref[0, rows, ln].astype(F32) * k_ref[0, rows, ln].astype(F32) * rk_ref[:, ln])
               for ln in lns]
        yc = _each(lambda y_: y_ - head_sum(y_) * (1.0 / HEAD64), y)
        var = _each(lambda c: head_sum(c * c) * (1.0 / HEAD64), yc)
        for pr, ln in enumerate(lns):
            out = (yc[pr] * lax.rsqrt(var[pr] + RWKV_GN_EPS) * lng_ref[:, ln] + lnb_ref[:, ln]
                   + rkk[pr] * v_ref[0, rows, ln].astype(F32))
            o_ref[0, rows, ln] = out.astype(o_ref.dtype)
        return carry

    lax.fori_loop(0, ts // (2 * CHUNK), finish, 0)


def _rwkv(p_rwkv, mu, w0, w_up, a0, a_up, k_k, k_a, r_k, ln_g, ln_b, tm=256, ts=1024, npairs=4, unroll=2):
    bsz, seq, _ = p_rwkv.shape
    tm, ts = min(tm, seq), min(ts, seq)
    unroll = min(unroll, ts // CHUNK)
    width = npairs * LANES
    nb = GROUP // width
    zeros = jnp.zeros((LORA, GROUP), F32)
    wup = jnp.concatenate([w_up, zeros], axis=0).astype(BF16)
    aup = jnp.concatenate([zeros, a_up], axis=0).astype(BF16)
    row = lambda v: v.reshape(1, -1)
    full = lambda n: pl.BlockSpec((1, n), lambda b, i: (0, 0))
    out_spec = pl.BlockSpec((1, tm, GROUP), lambda b, i: (b, i, 0))
    seq_f32 = jax.ShapeDtypeStruct((bsz, seq, GROUP), F32)
    seq_b16 = jax.ShapeDtypeStruct((bsz, seq, GROUP), BF16)
    r, lw, k, v, kk, kka = pl.pallas_call(
        _rwkv_prep_body, grid=(bsz, seq // tm),
        in_specs=[pl.BlockSpec((1, tm, N_RWKV), lambda b, i: (b, i, 0)),
                  pl.BlockSpec((1, 8, N_RWKV), lambda b, i: (b, jnp.maximum(i * (tm // 8) - 1, 0), 0)),
                  full(N_RWKV), full(GROUP),
                  pl.BlockSpec((2 * LORA, GROUP), lambda b, i: (0, 0)), full(GROUP),
                  pl.BlockSpec((2 * LORA, GROUP), lambda b, i: (0, 0)), full(GROUP), full(GROUP)],
        out_specs=[out_spec] * 6, out_shape=[seq_b16, seq_f32, seq_b16, seq_b16, seq_b16, seq_b16],
        compiler_params=_params(2),
        name="rwkv_prep")(p_rwkv, p_rwkv, row(mu), row(w0), wup, row(a0), aup, row(k_k), row(k_a))
    n_chunks = ts // CHUNK
    seq_spec = pl.BlockSpec((1, ts, width), lambda b, h, s: (b, s, h))
    par_spec = pl.BlockSpec((1, width), lambda b, h, s: (0, h))
    return pl.pallas_call(
        functools.partial(_rwkv_body, ts=ts, npairs=npairs, unroll=unroll), grid=(bsz, nb, seq // ts),
        in_specs=[seq_spec] * 6 + [par_spec] * 3,
        out_specs=seq_spec, out_shape=jax.ShapeDtypeStruct((bsz, seq, GROUP), BF16),
        scratch_shapes=[pltpu.VMEM((npairs, ts, LANES), BF16), pltpu.VMEM((npairs, ts, LANES), F32),
                        pltpu.VMEM((npairs, n_chunks * LANES, LANES), BF16),
                        pltpu.VMEM((npairs, n_chunks * LANES, LANES), F32),
                        pltpu.VMEM((npairs, n_chunks * LANES, LANES), F32),
                        pltpu.VMEM((npairs, LANES, LANES), F32)],
        compiler_params=_params(3), name="rwkv7")(
            r, lw, k, v, kk, kka, row(r_k), row(ln_g), row(ln_b))


def _split_w_in(w_in):
    b0, b1, b2, b3 = N_RWKV, N_RWKV + N_FOX, N_RWKV + N_FOX + N_GDN, N_RWKV + N_FOX + N_GDN + N_DIFF
    pad = lambda w, n: jnp.pad(w, ((0, 0), (0, 0), (0, n - w.shape[-1])))
    cols = lambda lo, hi: w_in[..., lo:hi].astype(BF16)
    w_a = jnp.concatenate([cols(0, b0), pad(cols(b0, b1), SEG_PAD)], axis=-1)
    w_b = jnp.concatenate([pad(cols(b1, b2), GDN_PAD), cols(b2, b3)], axis=-1)
    return w_a, w_b, cols(b3, w_in.shape[-1])


def kernel(x, norm_g, w_in, w_out, rwkv_mu, rwkv_w0, rwkv_w_up, rwkv_a0, rwkv_a_up, rwkv_k_k, rwkv_k_a, rwkv_r_k, rwkv_ln_g, rwkv_ln_b, fox_q_g, fox_k_g, fox_f_b, gdn_conv, gdn_a_log, gdn_dt_bias, gdn_norm_g, diff_q_g, diff_k_g, diff_lq1, diff_lk1, diff_lq2, diff_lk2, diff_subln_g):
    bsz, seq, d = x.shape
    m = bsz * seq
    w_a, w_b, w_z = _split_w_in(w_in)
    w_out_b = w_out.astype(BF16)
    x2d = x.reshape(m, d)
    for l in range(w_in.shape[0]):
        h = _rmsnorm(x2d, norm_g[l])
        p_a = _matmul(h, w_a[l], IN_PROJ_TN, F32).reshape(bsz, seq, -1)
        p_b = _matmul(h, w_b[l], IN_PROJ_TN, F32).reshape(bsz, seq, -1)
        sz = _matmul(h, w_z[l], 1024, BF16, silu=True, name="in_proj_gate")
        y_rwkv = _rwkv(p_a, rwkv_mu[l], rwkv_w0[l], rwkv_w_up[l], rwkv_a0[l], rwkv_a_up[l], rwkv_k_k[l],
                       rwkv_k_a[l], rwkv_r_k[l].reshape(-1), rwkv_ln_g[l], rwkv_ln_b[l])
        y_fox = _fox(p_a, fox_f_b[l], fox_q_g[l], fox_k_g[l], col0=N_RWKV)
        y_gdn = _gdn(p_b, gdn_conv[l], gdn_a_log[l], gdn_dt_bias[l], gdn_norm_g[l])
        y_diff = _diff(p_b, l, diff_q_g[l], diff_k_g[l], diff_lq1[l], diff_lk1[l], diff_lq2[l], diff_lk2[l],
                       diff_subln_g[l], col0=GDN_PAD)
        ys = [y.reshape(m, GROUP) for y in (y_rwkv, y_fox, y_gdn, y_diff)]
        x2d = _outproj(x2d, ys, sz, w_out_b[l])
    return x2d.reshape(bsz, seq, d)
```
